```python
import numpy as np
import jax
import jax.numpy as jnp
from jax import lax

D_MODEL = 1024
BATCH = 32
SEQ = 2048
DEPTH = 2

HEAD_DIM = 64
GRID_W = 64
ROPE_THETA = 10000.0
EPS = 1e-6
NEG_INF = -1e30

A_HEADS = 8
NA_WIN_ROWS = 8
NA_WIN_COLS = 16
NA_Q_COLS = 16
NA_KEY_COLS = NA_Q_COLS + NA_WIN_COLS

B_HEADS = 8
B_KV_HEADS = 2
B_WINDOW = 128
B_BLOCK = 128

C_HEADS = 16
C_KV_HEADS = 4
C_BLOCK = 128

N_EXPERTS = 16
EC_CAPACITY_FACTOR = 2
D_FF = 2048

N_EVEN = (DEPTH + 1) // 2
N_ODD = DEPTH // 2

A_W = A_HEADS * HEAD_DIM
B_QW = B_HEADS * HEAD_DIM
B_KVW = B_KV_HEADS * HEAD_DIM
EVEN_IN = 3 * A_W + B_QW + 2 * B_KVW
EVEN_OUT = A_W + B_QW
C_QW = C_HEADS * HEAD_DIM
C_KVW = C_KV_HEADS * HEAD_DIM
ODD_IN = C_QW + 2 * C_KVW
ODD_OUT = C_QW

kernel_name = "hybrid_na_swa_axial_ec_moe_encoder"


def rms_norm(x, gain):
    xf = x.astype(jnp.float32)
    y = xf * lax.rsqrt(jnp.mean(xf * xf, axis=-1, keepdims=True) + EPS)
    return (y * gain.astype(jnp.float32)).astype(x.dtype)


def rope(x, pos):
    d = x.shape[-1]
    inv_freq = jnp.power(jnp.float32(ROPE_THETA), -jnp.arange(0, d, 2, dtype=jnp.float32) / d)
    ang = pos.astype(jnp.float32)[:, None] * inv_freq[None, :]
    cos = jnp.cos(ang)[None, :, None, :]
    sin = jnp.sin(ang)[None, :, None, :]
    x1, x2 = jnp.split(x.astype(jnp.float32), 2, axis=-1)
    return jnp.concatenate([x1 * cos - x2 * sin, x1 * sin + x2 * cos], axis=-1).astype(x.dtype)


def axial_rope(x, t):
    half = x.shape[-1] // 2
    return jnp.concatenate([rope(x[..., :half], t // GRID_W), rope(x[..., half:], t % GRID_W)], axis=-1)


def _na_column_tables():
    n_chunks = GRID_W // NA_Q_COLS
    j = np.arange(n_chunks)
    kc0 = np.clip(j * NA_Q_COLS - NA_WIN_COLS // 2, 0, GRID_W - NA_KEY_COLS)
    key_cols = kc0[:, None] + np.arange(NA_KEY_COLS)[None, :]
    q_cols = j[:, None] * NA_Q_COLS + np.arange(NA_Q_COLS)[None, :]
    start = np.clip(q_cols - NA_WIN_COLS // 2, 0, GRID_W - NA_WIN_COLS)
    rel = key_cols[:, None, :] - start[:, :, None]
    in_win = (rel >= 0) & (rel < NA_WIN_COLS)
    assert np.all(in_win.sum(-1) == NA_WIN_COLS)
    dc = key_cols[:, None, :] - q_cols[:, :, None]
    dc_idx = np.clip(dc, -(NA_WIN_COLS - 1), NA_WIN_COLS - 1) + NA_WIN_COLS - 1
    return key_cols, in_win, dc_idx


def neighbourhood_attention(q, k, v, rel_bias):
    b, s, h, d = q.shape
    rows = s // GRID_W
    wr = min(NA_WIN_ROWS, rows)
    key_cols, in_win, dc_idx = _na_column_tables()
    nc = key_cols.shape[0]
    qg = q.reshape(b, rows, GRID_W, h, d)
    kg = k.reshape(b, rows, GRID_W, h, d)
    vg = v.reshape(b, rows, GRID_W, h, d)
    col_bias = rel_bias[:, :, dc_idx]
    mask = jnp.asarray(in_win)[:, :, None, :]
    scale = HEAD_DIM ** -0.5

    def one_row(r):
        rs = jnp.clip(r - wr // 2, 0, rows - wr)
        k_rows = lax.dynamic_slice_in_dim(kg, rs, wr, axis=1)
        v_rows = lax.dynamic_slice_in_dim(vg, rs, wr, axis=1)
        kb = k_rows[:, :, key_cols]
        vb = v_rows[:, :, key_cols]
        qb = lax.dynamic_index_in_dim(qg, r, axis=1, keepdims=False).reshape(b, nc, NA_Q_COLS, h, d)
        sc = jnp.einsum('bjqhd,brjkhd->bhjqrk', qb, kb, preferred_element_type=jnp.float32) * scale
        dr_idx = rs + jnp.arange(wr) - r + NA_WIN_ROWS - 1
        bias = jnp.take(col_bias, dr_idx, axis=1).transpose(0, 2, 3, 1, 4)
        sc = jnp.where(mask, sc + bias[None].astype(jnp.float32), NEG_INF)
        p = jax.nn.softmax(sc, axis=(-2, -1))
        o = jnp.einsum('bhjqrk,brjkhd->bjqhd', p.astype(v.dtype), vb)
        return o.reshape(b, GRID_W, h * d)

    out = lax.map(one_row, jnp.arange(rows))
    return jnp.transpose(out, (1, 0, 2, 3)).reshape(b, s, h * d)


def windowed_sink_attention(q, k, v, sink):
    b, s, hq, d = q.shape
    hkv = k.shape[2]
    g = hq // hkv
    nb = s // B_BLOCK
    pad = ((0, 0), (B_BLOCK, B_BLOCK), (0, 0), (0, 0))
    kp = jnp.pad(k, pad)
    vp = jnp.pad(v, pad)
    sink_hg = sink.astype(jnp.float32).reshape(hkv, g)[None, :, :, None, None]
    scale = HEAD_DIM ** -0.5

    def one_block(i):
        q_blk = lax.dynamic_slice_in_dim(q, i * B_BLOCK, B_BLOCK, axis=1).reshape(b, B_BLOCK, hkv, g, d)
        k_blk = lax.dynamic_slice_in_dim(kp, i * B_BLOCK, 3 * B_BLOCK, axis=1)
        v_blk = lax.dynamic_slice_in_dim(vp, i * B_BLOCK, 3 * B_BLOCK, axis=1)
        sc = jnp.einsum('bqhgd,bkhd->bhgqk', q_blk, k_blk, preferred_element_type=jnp.float32) * scale
        qpos = i * B_BLOCK + jnp.arange(B_BLOCK)
        kpos = (i - 1) * B_BLOCK + jnp.arange(3 * B_BLOCK)
        valid = (kpos[None, :] >= 0) & (kpos[None, :] < s) & (jnp.abs(qpos[:, None] - kpos[None, :]) <= B_WINDOW)
        sc = jnp.where(valid, sc, NEG_INF)
        m = jnp.maximum(jnp.max(sc, axis=-1, keepdims=True), sink_hg)
        e = jnp.exp(sc - m)
        p = e / (jnp.sum(e, axis=-1, keepdims=True) + jnp.exp(sink_hg - m))
        o = jnp.einsum('bhgqk,bkhd->bqhgd', p.astype(v.dtype), v_blk)
        return o.reshape(b, B_BLOCK, hq * d)

    out = lax.map(one_block, jnp.arange(nb))
    return jnp.transpose(out, (1, 0, 2, 3)).reshape(b, s, hq * d)


def dense_block_attention(q, k, v):
    b, s, hq, d = q.shape
    hkv = k.shape[2]
    g = hq // hkv
    nb = s // C_BLOCK
    scale = HEAD_DIM ** -0.5

    def one_block(i):
        q_blk = lax.dynamic_slice_in_dim(q, i * C_BLOCK, C_BLOCK, axis=1).reshape(b, C_BLOCK, hkv, g, d)
        sc = jnp.einsum('bqhgd,bkhd->bhgqk', q_blk, k, preferred_element_type=jnp.float32) * scale
        p = jax.nn.softmax(sc, axis=-1)
        o = jnp.einsum('bhgqk,bkhd->bqhgd', p.astype(v.dtype), v)
        return o.reshape(b, C_BLOCK, hq * d)

    out = lax.map(one_block, jnp.arange(nb))
    return jnp.transpose(out, (1, 0, 2, 3)).reshape(b, s, hq * d)


def expert_choice_moe(x, w_router, w_gate, w_up, w_down):
    b, s, dm = x.shape
    cap = EC_CAPACITY_FACTOR * s // N_EXPERTS
    logits = jnp.einsum('bsd,de->bse', x, w_router, preferred_element_type=jnp.float32)
    affinity = jax.nn.softmax(logits, axis=-1)
    gate, idx = lax.top_k(jnp.swapaxes(affinity, 1, 2), cap)

    def run_expert(args):
        wg, wu, wd, ids, gt = args
        xe = jax.vmap(lambda xb, ib: xb[ib])(x, ids)
        hdn = jax.nn.silu(xe @ wg) * (xe @ wu)
        return (hdn @ wd) * gt[..., None].astype(x.dtype)

    ye = lax.map(run_expert, (w_gate, w_up, w_down, jnp.swapaxes(idx, 0, 1), jnp.swapaxes(gate, 0, 1)))
    ye = jnp.swapaxes(ye, 0, 1).reshape(b, N_EXPERTS * cap, dm)
    flat_idx = idx.reshape(b, N_EXPERTS * cap)
    return jax.vmap(lambda ib, yb: jnp.zeros((s, dm), yb.dtype).at[ib].add(yb))(flat_idx, ye)


def setup_inputs(seed: int = 0) -> dict:
    key = jax.random.key(seed)
    ks = jax.random.split(key, 20)
    nrm = jax.random.normal
    f32 = jnp.float32
    return {
        "x": nrm(ks[0], (BATCH, SEQ, D_MODEL), f32),
        "attn_norm_even": 1.0 + 0.02 * nrm(ks[1], (N_EVEN, D_MODEL), f32),
        "w_in_even": nrm(ks[2], (N_EVEN, D_MODEL, EVEN_IN), f32) * D_MODEL ** -0.5,
        "q_norm_a": 1.0 + 0.02 * nrm(ks[3], (N_EVEN, HEAD_DIM), f32),
        "k_norm_a": 1.0 + 0.02 * nrm(ks[4], (N_EVEN, HEAD_DIM), f32),
        "rel_bias_a": 0.1 * nrm(ks[5], (N_EVEN, A_HEADS, 2 * NA_WIN_ROWS - 1, 2 * NA_WIN_COLS - 1), f32),
        "q_norm_b": 1.0 + 0.02 * nrm(ks[6], (N_EVEN, HEAD_DIM), f32),
        "k_norm_b": 1.0 + 0.02 * nrm(ks[7], (N_EVEN, HEAD_DIM), f32),
        "sink_b": 0.5 * nrm(ks[8], (N_EVEN, B_HEADS), f32),
        "w_out_even": nrm(ks[9], (N_EVEN, EVEN_OUT, D_MODEL), f32) * EVEN_OUT ** -0.5,
        "attn_norm_odd": 1.0 + 0.02 * nrm(ks[10], (N_ODD, D_MODEL), f32),
        "w_in_odd": nrm(ks[11], (N_ODD, D_MODEL, ODD_IN), f32) * D_MODEL ** -0.5,
        "q_norm_c": 1.0 + 0.02 * nrm(ks[12], (N_ODD, HEAD_DIM), f32),
        "k_norm_c": 1.0 + 0.02 * nrm(ks[13], (N_ODD, HEAD_DIM), f32),
        "w_out_odd": nrm(ks[14], (N_ODD, ODD_OUT, D_MODEL), f32) * ODD_OUT ** -0.5,
        "ffn_norm": 1.0 + 0.02 * nrm(ks[15], (DEPTH, D_MODEL), f32),
        "w_router": nrm(ks[16], (DEPTH, D_MODEL, N_EXPERTS), f32) * D_MODEL ** -0.5,
        "w_gate": nrm(ks[17], (DEPTH, N_EXPERTS, D_MODEL, D_FF), f32) * D_MODEL ** -0.5,
        "w_up": nrm(ks[18], (DEPTH, N_EXPERTS, D_MODEL, D_FF), f32) * D_MODEL ** -0.5,
        "w_down": nrm(ks[19], (DEPTH, N_EXPERTS, D_FF, D_MODEL), f32) * D_FF ** -0.5,
    }


def reference(x, attn_norm_even, w_in_even, q_norm_a, k_norm_a, rel_bias_a, q_norm_b, k_norm_b, sink_b,
              w_out_even, attn_norm_odd, w_in_odd, q_norm_c, k_norm_c, w_out_odd,
              ffn_norm, w_router, w_gate, w_up, w_down):
    b, s, _ = x.shape
    t = jnp.arange(s)
    even_splits = [A_W, 2 * A_W, 3 * A_W, 3 * A_W + B_QW, 3 * A_W + B_QW + B_KVW]
    odd_splits = [C_QW, C_QW + C_KVW]
    for layer in range(DEPTH):
        i = layer // 2
        if layer % 2 == 0:
            h = rms_norm(x, attn_norm_even[i])
            proj = h @ w_in_even[i]
            qa, ka, va, qb, kb, vb = jnp.split(proj, even_splits, axis=-1)
            qa = rms_norm(qa.reshape(b, s, A_HEADS, HEAD_DIM), q_norm_a[i])
            ka = rms_norm(ka.reshape(b, s, A_HEADS, HEAD_DIM), k_norm_a[i])
            va = va.reshape(b, s, A_HEADS, HEAD_DIM)
            out_a = neighbourhood_attention(qa, ka, va, rel_bias_a[i])
            qb = rope(rms_norm(qb.reshape(b, s, B_HEADS, HEAD_DIM), q_norm_b[i]), t)
            kb = rope(rms_norm(kb.reshape(b, s, B_KV_HEADS, HEAD_DIM), k_norm_b[i]), t)
            vb = vb.reshape(b, s, B_KV_HEADS, HEAD_DIM)
            out_b = windowed_sink_attention(qb, kb, vb, sink_b[i])
            x = x + jnp.concatenate([out_a, out_b], axis=-1) @ w_out_even[i]
        else:
            h = rms_norm(x, attn_norm_odd[i])
            proj = h @ w_in_odd[i]
            qc, kc, vc = jnp.split(proj, odd_splits, axis=-1)
            qc = axial_rope(rms_norm(qc.reshape(b, s, C_HEADS, HEAD_DIM), q_norm_c[i]), t)
            kc = axial_rope(rms_norm(kc.reshape(b, s, C_KV_HEADS, HEAD_DIM), k_norm_c[i]), t)
            vc = vc.reshape(b, s, C_KV_HEADS, HEAD_DIM)
            x = x + dense_block_attention(qc, kc, vc) @ w_out_odd[i]
        x = x + expert_choice_moe(rms_norm(x, ffn_norm[layer]), w_router[layer], w_gate[layer],
                                  w_up[layer], w_down[layer])
    return x
```

```python
import functools

import numpy as np
import jax
import jax.numpy as jnp
from jax import lax
from jax.experimental import pallas as pl
from jax.experimental.pallas import tpu as pltpu

D_MODEL = 1024
HEAD_DIM = 64
GRID_W = 64
ROPE_THETA = 10000.0
EPS = 1e-6
NEG_INF = -1e30

A_HEADS = 8
NA_WIN_ROWS = 8
NA_WIN_COLS = 16
B_HEADS = 8
B_KV_HEADS = 2
B_WINDOW = 128
C_HEADS = 16
C_KV_HEADS = 4
N_EXPERTS = 16
EC_CAPACITY_FACTOR = 2
D_FF = 2048

A_W = A_HEADS * HEAD_DIM
B_QW = B_HEADS * HEAD_DIM
B_KVW = B_KV_HEADS * HEAD_DIM
C_QW = C_HEADS * HEAD_DIM
C_KVW = C_KV_HEADS * HEAD_DIM

SUBLANES = 8
LANES = 128
ROW_TILES = D_MODEL // LANES
VMEM_LIMIT = 56 * 1024 * 1024

TM = 512
NA_QROWS = 4
NA_KROWS = 12
SWA_Q = 256
SWA_K = 512
DENSE_Q = 256

F32 = jnp.float32
BF16 = jnp.bfloat16


def _cparams(sem):
    return pltpu.CompilerParams(dimension_semantics=sem, vmem_limit_bytes=VMEM_LIMIT)


def _to_rowtile(src_ref, dst_ref, rows):
    def body(i, carry):
        r0 = pl.multiple_of(i * SUBLANES, SUBLANES)
        base = i * (SUBLANES * ROW_TILES)
        for c in range(ROW_TILES):
            dst_ref[pl.ds(base + c, SUBLANES, stride=ROW_TILES), :] = (
                src_ref[pl.ds(r0, SUBLANES), c * LANES:(c + 1) * LANES])
        return carry
    lax.fori_loop(0, rows // SUBLANES, body, 0)


def _from_rowtile(src_ref, dst_ref, rows):
    def body(i, carry):
        r0 = pl.multiple_of(i * SUBLANES, SUBLANES)
        base = i * (SUBLANES * ROW_TILES)
        for c in range(ROW_TILES):
            dst_ref[pl.ds(r0, SUBLANES), c * LANES:(c + 1) * LANES] = (
                src_ref[pl.ds(base + c, SUBLANES, stride=ROW_TILES), :])
        return carry
    lax.fori_loop(0, rows // SUBLANES, body, 0)


def _rms_rows(x):
    return x * lax.rsqrt(jnp.mean(x * x, axis=-1, keepdims=True) + EPS)


def _norm_proj_kernel(*refs, sections, has_moe, half):
    it = iter(refs)
    x_ref = next(it)
    moe_ref = next(it) if has_moe else None
    gain_ref = next(it)
    w_ref = next(it)
    cg_ref = next(it)
    cos_ref = next(it)
    sin_ref = next(it)
    bd_ref = next(it)
    n_out = len({s[3] for s in sections})
    outs = [next(it) for _ in range(n_out)]
    xo_ref = next(it) if has_moe else None
    xs_ref = next(it) if has_moe else None

    if has_moe:
        _from_rowtile(moe_ref, xs_ref, TM)
        x = x_ref[...] + xs_ref[...]
        xo_ref[...] = x
    else:
        x = x_ref[...]
    hb = (_rms_rows(x) * gain_ref[...]).astype(BF16)

    for (c0, width, kind, oi, oc) in sections:
        y = jnp.dot(hb, w_ref[:, c0:c0 + width], preferred_element_type=F32)
        if kind != "plain":
            y2 = y * y
            hi = y2.astype(BF16)
            lo = (y2 - hi.astype(F32)).astype(BF16)
            bd = bd_ref[:width, :width]
            ss = (jnp.dot(hi, bd, preferred_element_type=F32)
                  + jnp.dot(lo, bd, preferred_element_type=F32))
            y = y * lax.rsqrt(ss + EPS) * cg_ref[:, c0:c0 + width]
        if kind == "rope":
            lane = lax.broadcasted_iota(jnp.int32, (1, width), 1)
            first = (lane % (2 * half)) < half
            part = jnp.where(first, pltpu.roll(y, width - half, 1), pltpu.roll(y, half, 1))
            y = y * cos_ref[:, :width] + part * sin_ref[:, :width]
        outs[oi][:, oc:oc + width] = y.astype(BF16)


def _norm_proj(x2d, moe_rt, gain, w_bf16, colgain, cos_t, sin_t, sections, out_widths, half, seq):
    m = x2d.shape[0]
    n_in = w_bf16.shape[1]
    has_moe = moe_rt is not None
    bd = jnp.asarray(np.kron(np.eye(4), np.full((HEAD_DIM, HEAD_DIM), 1.0 / HEAD_DIM)), BF16)
    tiles_per_seq = seq // TM
    in_specs = [pl.BlockSpec((TM, D_MODEL), lambda i: (i, 0))]
    args = [x2d]
    if has_moe:
        in_specs.append(pl.BlockSpec((TM * ROW_TILES, LANES), lambda i: (i, 0)))
        args.append(moe_rt)
    in_specs += [
        pl.BlockSpec((1, D_MODEL), lambda i: (0, 0)),
        pl.BlockSpec((D_MODEL, n_in), lambda i: (0, 0)),
        pl.BlockSpec((1, n_in), lambda i: (0, 0)),
        pl.BlockSpec((TM, 256), lambda i: (i % tiles_per_seq, 0)),
        pl.BlockSpec((TM, 256), lambda i: (i % tiles_per_seq, 0)),
        pl.BlockSpec((256, 256), lambda i: (0, 0)),
    ]
    args += [gain.reshape(1, D_MODEL), w_bf16, colgain, cos_t, sin_t, bd]
    out_shape = [jax.ShapeDtypeStruct((m, wd), BF16) for wd in out_widths]
    out_specs = [pl.BlockSpec((TM, wd), lambda i: (i, 0)) for wd in out_widths]
    scratch = []
    if has_moe:
        out_shape.append(jax.ShapeDtypeStruct((m, D_MODEL), F32))
        out_specs.append(pl.BlockSpec((TM, D_MODEL), lambda i: (i, 0)))
        scratch.append(pltpu.VMEM((TM, D_MODEL), F32))
    kern = functools.partial(_norm_proj_kernel, sections=tuple(sections), has_moe=has_moe, half=half)
    return pl.pallas_call(
        kern, grid=(m // TM,), in_specs=in_specs, out_specs=out_specs, out_shape=out_shape,
        scratch_shapes=scratch, compiler_params=_cparams(("parallel",)),
        name="norm_proj_moe" if has_moe else "norm_proj")(*args)


def _na_kernel(q_ref, k_ref, v_ref, bias_ref, o_ref):
    nq = NA_QROWS * GRID_W
    nk = NA_KROWS * GRID_W
    rows = q_ref.shape[1] // GRID_W
    n_blocks = rows // NA_QROWS
    lane = lax.broadcasted_iota(jnp.int32, (1, LANES), 1)
    head0 = lane < HEAD_DIM

    def body(qb, carry):
        r0 = qb * NA_QROWS
        start = jnp.clip(r0 - NA_WIN_ROWS // 2, 0, rows - NA_KROWS)
        pat = jnp.where(qb == 0, 0, jnp.where(qb == n_blocks - 1, 2, 1))
        qs = pl.multiple_of(r0 * GRID_W, GRID_W)
        ks = pl.multiple_of(start * GRID_W, GRID_W)
        q = q_ref[0, pl.ds(qs, nq), :]
        k = k_ref[0, pl.ds(ks, nk), :]
        v = v_ref[0, pl.ds(ks, nk), :]
        acc = None
        for hh in range(2):
            sel = head0 if hh == 0 else jnp.logical_not(head0)
            qm = jnp.where(sel, q, jnp.zeros_like(q))
            s = lax.dot_general(qm, k, (((1,), (1,)), ((), ())), preferred_element_type=F32)
            s = s + bias_ref[pat, hh]
            m = jnp.max(s, axis=-1, keepdims=True)
            p = jnp.exp(s - m)
            l = jnp.sum(p, axis=-1, keepdims=True)
            o = jnp.dot(p.astype(BF16), v, preferred_element_type=F32) / l
            acc = o if acc is None else jnp.where(sel, o, acc)
        o_ref[0, pl.ds(qs, nq), :] = acc.astype(BF16)
        return carry

    lax.fori_loop(0, n_blocks, body, 0)


def _na_bias_tables(rel_bias, rows):
    nq, nk = NA_QROWS * GRID_W, NA_KROWS * GRID_W
    tabs_dr, tabs_dc, tabs_in = [], [], []
    for r0 in (0, 2 * NA_QROWS, rows - NA_QROWS):
        start = int(np.clip(r0 - NA_WIN_ROWS // 2, 0, rows - NA_KROWS))
        r = r0 + np.arange(nq) // GRID_W
        c = np.arange(nq) % GRID_W
        kr = start + np.arange(nk) // GRID_W
        kc = np.arange(nk) % GRID_W
        rs = np.clip(r - NA_WIN_ROWS // 2, 0, rows - NA_WIN_ROWS)
        cs = np.clip(c - NA_WIN_COLS // 2, 0, GRID_W - NA_WIN_COLS)
        in_r = (kr[None, :] >= rs[:, None]) & (kr[None, :] < rs[:, None] + NA_WIN_ROWS)
        in_c = (kc[None, :] >= cs[:, None]) & (kc[None, :] < cs[:, None] + NA_WIN_COLS)
        dr = np.clip(kr[None, :] - r[:, None], -(NA_WIN_ROWS - 1), NA_WIN_ROWS - 1) + NA_WIN_ROWS - 1
        dc = np.clip(kc[None, :] - c[:, None], -(NA_WIN_COLS - 1), NA_WIN_COLS - 1) + NA_WIN_COLS - 1
        tabs_dr.append(dr)
        tabs_dc.append(dc)
        tabs_in.append(in_r & in_c)
    dr = np.stack(tabs_dr)
    dc = np.stack(tabs_dc)
    inw = np.stack(tabs_in)
    vals = rel_bias.astype(F32)[:, dr, dc]
    tab = jnp.where(jnp.asarray(inw)[None], vals, NEG_INF)
    return jnp.transpose(tab, (1, 0, 2, 3))


def _na_attention(q, k, v, rel_bias):
    b, s, _ = q.shape
    rows = s // GRID_W
    assert rows % NA_QROWS == 0 and rows >= NA_KROWS and rows // NA_QROWS >= 3
    nq, nk = NA_QROWS * GRID_W, NA_KROWS * GRID_W
    bias = _na_bias_tables(rel_bias, rows)
    blk = pl.BlockSpec((1, s, LANES), lambda bi, hp: (bi, 0, hp))
    return pl.pallas_call(
        _na_kernel, grid=(b, A_HEADS // 2),
        in_specs=[blk, blk, blk, pl.BlockSpec((3, 2, nq, nk), lambda bi, hp: (0, hp, 0, 0))],
        out_specs=blk, out_shape=jax.ShapeDtypeStruct((b, s, A_W), BF16),
        compiler_params=_cparams(("parallel", "arbitrary")), name="na_attention")(q, k, v, bias)


def _swa_kernel(sink_ref, q_ref, k_ref, v_ref, rep_ref, o_ref, kt_ref, vt_ref):
    s_len = q_ref.shape[1]
    n_groups = B_HEADS // B_KV_HEADS
    gw = n_groups * HEAD_DIM
    lane = lax.broadcasted_iota(jnp.int32, (1, gw), 1)

    for g in range(B_KV_HEADS):
        kt_ref[...] = jnp.dot(k_ref[0], rep_ref[g], preferred_element_type=F32).astype(BF16)
        vt_ref[...] = jnp.dot(v_ref[0], rep_ref[g], preferred_element_type=F32).astype(BF16)

        def body(i, carry):
            q0 = pl.multiple_of(i * SWA_Q, SWA_Q)
            k0 = pl.multiple_of(jnp.clip(i * SWA_Q - B_WINDOW, 0, s_len - SWA_K), B_WINDOW)
            q = q_ref[0, pl.ds(q0, SWA_Q), g * gw:(g + 1) * gw]
            kt = kt_ref[pl.ds(k0, SWA_K), :]
            vt = vt_ref[pl.ds(k0, SWA_K), :]
            qpos = q0 + lax.broadcasted_iota(jnp.int32, (SWA_Q, SWA_K), 0)
            kpos = k0 + lax.broadcasted_iota(jnp.int32, (SWA_Q, SWA_K), 1)
            valid = jnp.abs(qpos - kpos) <= B_WINDOW
            acc = jnp.zeros((SWA_Q, gw), F32)
            for j in range(n_groups):
                sel = (lane >= j * HEAD_DIM) & (lane < (j + 1) * HEAD_DIM)
                sink = sink_ref[g * n_groups + j]
                qm = jnp.where(sel, q, jnp.zeros_like(q))
                s = lax.dot_general(qm, kt, (((1,), (1,)), ((), ())), preferred_element_type=F32)
                s = jnp.where(valid, s, NEG_INF)
                m = jnp.maximum(jnp.max(s, axis=-1, keepdims=True), sink)
                e = jnp.exp(s - m)
                den = jnp.sum(e, axis=-1, keepdims=True) + jnp.exp(sink - m)
                o = jnp.dot(e.astype(BF16), vt, preferred_element_type=F32) / den
                acc = jnp.where(sel, o, acc)
            o_ref[0, pl.ds(q0, SWA_Q), g * gw:(g + 1) * gw] = acc.astype(BF16)
            return carry

        lax.fori_loop(0, s_len // SWA_Q, body, 0)


def _replication_matrices(n_kv):
    rep = np.zeros((n_kv, n_kv * HEAD_DIM, 4 * HEAD_DIM), np.float32)
    for g in range(n_kv):
        for j in range(4 * HEAD_DIM):
            rep[g, g * HEAD_DIM + j % HEAD_DIM, j] = 1.0
    return jnp.asarray(rep, BF16)


def _swa_attention(q, k, v, sink):
    b, s, _ = q.shape
    assert s % SWA_Q == 0 and s >= SWA_K and SWA_K >= SWA_Q + 2 * B_WINDOW
    rep = _replication_matrices(B_KV_HEADS)
    return pl.pallas_call(
        _swa_kernel, grid=(b,),
        in_specs=[pl.BlockSpec(memory_space=pltpu.SMEM),
                  pl.BlockSpec((1, s, B_QW), lambda bi: (bi, 0, 0)),
                  pl.BlockSpec((1, s, B_KVW), lambda bi: (bi, 0, 0)),
                  pl.BlockSpec((1, s, B_KVW), lambda bi: (bi, 0, 0)),
                  pl.BlockSpec((B_KV_HEADS, B_KVW, 4 * HEAD_DIM), lambda bi: (0, 0, 0))],
        out_specs=pl.BlockSpec((1, s, B_QW), lambda bi: (bi, 0, 0)),
        out_shape=jax.ShapeDtypeStruct((b, s, B_QW), BF16),
        scratch_shapes=[pltpu.VMEM((s, 4 * HEAD_DIM), BF16), pltpu.VMEM((s, 4 * HEAD_DIM), BF16)],
        compiler_params=_cparams(("parallel",)), name="swa_attention")(sink.astype(F32), q, k, v, rep)


def _dense_kernel(q_ref, k_ref, v_ref, rep_ref, o_ref, kt_ref, vt_ref):
    s_len = q_ref.shape[1]
    gw = 4 * HEAD_DIM
    lane = lax.broadcasted_iota(jnp.int32, (1, gw), 1)
    kt_ref[...] = jnp.dot(k_ref[0], rep_ref[0], preferred_element_type=F32).astype(BF16)
    vt_ref[...] = jnp.dot(v_ref[0], rep_ref[0], preferred_element_type=F32).astype(BF16)

    def body(i, carry):
        q0 = pl.multiple_of(i * DENSE_Q, DENSE_Q)
        q = q_ref[0, pl.ds(q0, DENSE_Q), :]
        acc = jnp.zeros((DENSE_Q, gw), F32)
        for j in range(4):
            sel = (lane >= j * HEAD_DIM) & (lane < (j + 1) * HEAD_DIM)
            qm = jnp.where(sel, q, jnp.zeros_like(q))
            s = lax.dot_general(qm, kt_ref[...], (((1,), (1,)), ((), ())), preferred_element_type=F32)
            m = jnp.max(s, axis=-1, keepdims=True)
            p = jnp.exp(s - m)
            l = jnp.sum(p, axis=-1, keepdims=True)
            o = jnp.dot(p.astype(BF16), vt_ref[...], preferred_element_type=F32) / l
            acc = jnp.where(sel, o, acc)
        o_ref[0, pl.ds(q0, DENSE_Q), :] = acc.astype(BF16)
        return carry

    lax.fori_loop(0, s_len // DENSE_Q, body, 0)


def _dense_attention(q, k, v):
    b, s, _ = q.shape
    gw = 4 * HEAD_DIM
    rep = _replication_matrices(C_KV_HEADS)
    return pl.pallas_call(
        _dense_kernel, grid=(b, C_KV_HEADS),
        in_specs=[pl.BlockSpec((1, s, gw), lambda bi, g: (bi, 0, g)),
                  pl.BlockSpec((1, s, C_KVW), lambda bi, g: (bi, 0, 0)),
                  pl.BlockSpec((1, s, C_KVW), lambda bi, g: (bi, 0, 0)),
                  pl.BlockSpec((1, C_KVW, gw), lambda bi, g: (g, 0, 0))],
        out_specs=pl.BlockSpec((1, s, gw), lambda bi, g: (bi, 0, g)),
        out_shape=jax.ShapeDtypeStruct((b, s, C_QW), BF16),
        scratch_shapes=[pltpu.VMEM((s, gw), BF16), pltpu.VMEM((s, gw), BF16)],
        compiler_params=_cparams(("parallel", "arbitrary")), name="dense_attention")(q, k, v, rep)


def _out_proj_kernel(*refs, n_parts):
    it = iter(refs)
    x_ref = next(it)
    parts = [next(it) for _ in range(n_parts)]
    w_ref = next(it)
    gain_ref = next(it)
    wr_ref = next(it)
    wrt_ref = next(it)
    xo_ref = next(it)
    hrt_ref = next(it)
    aff_ref = next(it)
    afft_ref = next(it)
    h_ref = next(it)

    y = x_ref[...]
    c0 = 0
    for p_ref in parts:
        wp = p_ref.shape[1]
        y = y + jnp.dot(p_ref[...], w_ref[c0:c0 + wp, :], preferred_element_type=F32)
        c0 += wp
    xo_ref[...] = y
    h = _rms_rows(y) * gain_ref[...]
    h_ref[...] = h
    _to_rowtile(h_ref, hrt_ref, TM)

    lane = lax.broadcasted_iota(jnp.int32, (1, LANES), 1)
    logits = jnp.dot(h, wr_ref[...], preferred_element_type=F32, precision=lax.Precision.HIGHEST)
    logits = jnp.where(lane < N_EXPERTS, logits, NEG_INF)
    e = jnp.exp(logits - jnp.max(logits, axis=-1, keepdims=True))
    aff_ref[...] = e / jnp.sum(e, axis=-1, keepdims=True)

    lt = lax.dot_general(wrt_ref[...], h, (((1,), (1,)), ((), ())), preferred_element_type=F32,
                         precision=lax.Precision.HIGHEST)
    et = jnp.exp(lt - jnp.max(lt, axis=0, keepdims=True))
    afft_ref[0] = et / jnp.sum(et, axis=0, keepdims=True)


def _out_proj(x2d, parts, w_out_bf16, ffn_gain, w_router, batch, seq):
    m = x2d.shape[0]
    tiles_per_seq = seq // TM
    wr_pad = jnp.zeros((D_MODEL, LANES), F32).at[:, :N_EXPERTS].set(w_router.astype(F32))
    wr_t = jnp.transpose(w_router.astype(F32))
    in_specs = [pl.BlockSpec((TM, D_MODEL), lambda i: (i, 0))]
    in_specs += [pl.BlockSpec((TM, p.shape[1]), lambda i: (i, 0)) for p in parts]
    in_specs += [pl.BlockSpec((D_MODEL, D_MODEL), lambda i: (0, 0)),
                 pl.BlockSpec((1, D_MODEL), lambda i: (0, 0)),
                 pl.BlockSpec((D_MODEL, LANES), lambda i: (0, 0)),
                 pl.BlockSpec((N_EXPERTS, D_MODEL), lambda i: (0, 0))]
    out_shape = [jax.ShapeDtypeStruct((m, D_MODEL), F32),
                 jax.ShapeDtypeStruct((m * ROW_TILES, LANES), F32),
                 jax.ShapeDtypeStruct((m, LANES), F32),
                 jax.ShapeDtypeStruct((batch, N_EXPERTS, seq), F32)]
    out_specs = [pl.BlockSpec((TM, D_MODEL), lambda i: (i, 0)),
                 pl.BlockSpec((TM * ROW_TILES, LANES), lambda i: (i, 0)),
                 pl.BlockSpec((TM, LANES), lambda i: (i, 0)),
                 pl.BlockSpec((1, N_EXPERTS, TM), lambda i: (i // tiles_per_seq, 0, i % tiles_per_seq))]
    kern = functools.partial(_out_proj_kernel, n_parts=len(parts))
    return pl.pallas_call(
        kern, grid=(m // TM,), in_specs=in_specs, out_specs=out_specs, out_shape=out_shape,
        scratch_shapes=[pltpu.VMEM((TM, D_MODEL), F32)],
        compiler_params=_cparams(("parallel",)), name="out_proj_router")(
            x2d, *parts, w_out_bf16, ffn_gain.reshape(1, D_MODEL), wr_pad, wr_t)


def _topk_kernel(afft_ref, aff_ref, ids_ref, gate_ref, *, cap):
    s_len = afft_ref.shape[2]
    at = afft_ref[0]
    bits = pltpu.bitcast(at, jnp.int32)

    def count(mask):
        return jnp.sum(jnp.where(mask, 1.0, 0.0), axis=1, keepdims=True)

    def bit_step(k, thr):
        cand = thr | jnp.left_shift(jnp.int32(1), 30 - k)
        return jnp.where(count(bits >= cand) >= cap, cand, thr)

    thr = lax.fori_loop(0, 31, bit_step, jnp.zeros((N_EXPERTS, 1), jnp.int32))
    gt = bits > thr
    eq = bits == thr
    need = cap - count(gt)

    ri = lax.broadcasted_iota(jnp.int32, (LANES, LANES), 0)
    ci = lax.broadcasted_iota(jnp.int32, (LANES, LANES), 1)
    upper = jnp.where(ri < ci, 1.0, 0.0).astype(BF16)

    def prefix(mask_f32):
        out = []
        run = jnp.zeros((N_EXPERTS, 1), F32)
        for c in range(s_len // LANES):
            blk = mask_f32[:, c * LANES:(c + 1) * LANES]
            out.append(jnp.dot(blk.astype(BF16), upper, preferred_element_type=F32) + run)
            run = run + jnp.sum(blk, axis=1, keepdims=True)
        return out

    eq_f = jnp.where(eq, 1.0, 0.0)
    eq_rank = prefix(eq_f)
    sel_tiles = []
    for c in range(s_len // LANES):
        sl = slice(c * LANES, (c + 1) * LANES)
        sel_tiles.append(jnp.where(gt[:, sl] | (eq[:, sl] & (eq_rank[c] < need)), 1.0, 0.0))
    sel_f = jnp.concatenate(sel_tiles, axis=1)
    pos = jnp.concatenate(prefix(sel_f), axis=1)
    key = jnp.where(sel_f > 0.5, pos, -1.0)

    a = aff_ref[0]
    a_hi = a.astype(BF16).astype(F32)
    r1 = a - a_hi
    a_mid = r1.astype(BF16).astype(F32)
    a_lo = r1 - a_mid
    lane = lax.broadcasted_iota(jnp.int32, (s_len, LANES), 1)
    tok = lax.broadcasted_iota(jnp.int32, (s_len, LANES), 0)
    packed = a_hi + pltpu.roll(a_mid, N_EXPERTS, 1) + pltpu.roll(a_lo, 2 * N_EXPERTS, 1)
    packed = jnp.where(lane == 64, (tok // 64).astype(F32), packed)
    packed = jnp.where(lane == 65, (tok % 64).astype(F32), packed)
    rmat = packed.astype(BF16)

    slot = lax.broadcasted_iota(jnp.int32, (cap, 1), 0).astype(F32)
    for e in range(N_EXPERTS):
        onehot_t = jnp.where(key[e:e + 1, :] == slot, 1.0, 0.0).astype(BF16)
        res = jnp.dot(onehot_t, rmat, preferred_element_type=F32)
        ids_ref[0, e] = (res[:, 64:65] * 64.0 + res[:, 65:66]).astype(jnp.int32)
        gate_ref[0, e] = (res[:, e:e + 1] + res[:, N_EXPERTS + e:N_EXPERTS + e + 1]
                          + res[:, 2 * N_EXPERTS + e:2 * N_EXPERTS + e + 1])


def _topk(aff_t, aff, cap):
    b, _, s = aff_t.shape
    kern = functools.partial(_topk_kernel, cap=cap)
    return pl.pallas_call(
        kern, grid=(b,),
        in_specs=[pl.BlockSpec((1, N_EXPERTS, s), lambda bi: (bi, 0, 0)),
                  pl.BlockSpec((1, s, LANES), lambda bi: (bi, 0, 0))],
        out_specs=[pl.BlockSpec((1, N_EXPERTS, cap, 1), lambda bi: (bi, 0, 0, 0)),
                   pl.BlockSpec((1, N_EXPERTS, cap, 1), lambda bi: (bi, 0, 0, 0))],
        out_shape=[jax.ShapeDtypeStruct((b, N_EXPERTS, cap, 1), jnp.int32),
                   jax.ShapeDtypeStruct((b, N_EXPERTS, cap, 1), F32)],
        compiler_params=_cparams(("parallel",)), name="expert_topk")(aff_t, aff)


MOE_GROUP = 8


def _moe_kernel(ids_ref, gate_ref, h_ref, wg_ref, wu_ref, wd_ref, o_ref, g_ref, xe_ref, y_ref, *, cap):
    e = pl.program_id(1)
    f = pl.program_id(2)
    nf = pl.num_programs(2)

    @pl.when((e == 0) & (f == 0))
    def _():
        o_ref[...] = jnp.zeros_like(o_ref)

    @pl.when(f == 0)
    def _():
        def gather(jg, carry):
            for u in range(MOE_GROUP):
                j = jg * MOE_GROUP + u
                src = pl.multiple_of(ids_ref[0, 0, j] * ROW_TILES, ROW_TILES)
                dst = pl.multiple_of(j * ROW_TILES, ROW_TILES)
                g_ref[pl.ds(dst, ROW_TILES), :] = h_ref[0, pl.ds(src, ROW_TILES), :]
            return carry
        lax.fori_loop(0, cap // MOE_GROUP, gather, 0)
        _from_rowtile(g_ref, y_ref, cap)
        xe_ref[...] = y_ref[...].astype(BF16)

    xe = xe_ref[...]
    hg = jnp.dot(xe, wg_ref[0], preferred_element_type=F32)
    hu = jnp.dot(xe, wu_ref[0], preferred_element_type=F32)
    hid = (hg * (1.0 / (1.0 + jnp.exp(-hg))) * hu).astype(BF16)
    y = jnp.dot(hid, wd_ref[0], preferred_element_type=F32)

    @pl.when(f == 0)
    def _():
        y_ref[...] = y

    @pl.when(f > 0)
    def _():
        y_ref[...] += y

    @pl.when(f == nf - 1)
    def _():
        y_ref[...] = y_ref[...] * gate_ref[0]
        _to_rowtile(y_ref, g_ref, cap)

        def scatter(jg, carry):
            rows = []
            for u in range(MOE_GROUP):
                j = jg * MOE_GROUP + u
                rows.append(pl.multiple_of(ids_ref[0, 0, j] * ROW_TILES, ROW_TILES))
            vals = []
            for u in range(MOE_GROUP):
                j = jg * MOE_GROUP + u
                src = pl.multiple_of(j * ROW_TILES, ROW_TILES)
                vals.append(o_ref[0, pl.ds(rows[u], ROW_TILES), :] + g_ref[pl.ds(src, ROW_TILES), :])
            for u in range(MOE_GROUP):
                o_ref[0, pl.ds(rows[u], ROW_TILES), :] = vals[u]
            return carry
        lax.fori_loop(0, cap // MOE_GROUP, scatter, 0)


def _moe(ids, gates, h_rt, wg, wu, wd, cap):
    b = h_rt.shape[0]
    rt_rows = h_rt.shape[1]
    n_f = 2
    ft = D_FF // n_f
    kern = functools.partial(_moe_kernel, cap=cap)
    return pl.pallas_call(
        kern, grid=(b, N_EXPERTS, n_f),
        in_specs=[pl.BlockSpec((1, 1, cap), lambda bi, e, f: (bi * N_EXPERTS + e, 0, 0),
                               memory_space=pltpu.SMEM),
                  pl.BlockSpec((1, cap, 1), lambda bi, e, f: (bi * N_EXPERTS + e, 0, 0)),
                  pl.BlockSpec((1, rt_rows, LANES), lambda bi, e, f: (bi, 0, 0)),
                  pl.BlockSpec((1, D_MODEL, ft), lambda bi, e, f: (e, 0, f)),
                  pl.BlockSpec((1, D_MODEL, ft), lambda bi, e, f: (e, 0, f)),
                  pl.BlockSpec((1, ft, D_MODEL), lambda bi, e, f: (e, f, 0))],
        out_specs=pl.BlockSpec((1, rt_rows, LANES), lambda bi, e, f: (bi, 0, 0)),
        out_shape=jax.ShapeDtypeStruct((b, rt_rows, LANES), F32),
        scratch_shapes=[pltpu.VMEM((cap * ROW_TILES, LANES), F32),
                        pltpu.VMEM((cap, D_MODEL), BF16),
                        pltpu.VMEM((cap, D_MODEL), F32)],
        compiler_params=_cparams(("parallel", "arbitrary", "arbitrary")), name="moe_ffn")(
            ids, gates, h_rt, wg, wu, wd)


def _merge_kernel(x_ref, moe_ref, o_ref, xs_ref):
    _from_rowtile(moe_ref, xs_ref, TM)
    o_ref[...] = x_ref[...] + xs_ref[...]


def _merge(x2d, moe_rt):
    m = x2d.shape[0]
    return pl.pallas_call(
        _merge_kernel, grid=(m // TM,),
        in_specs=[pl.BlockSpec((TM, D_MODEL), lambda i: (i, 0)),
                  pl.BlockSpec((TM * ROW_TILES, LANES), lambda i: (i, 0))],
        out_specs=pl.BlockSpec((TM, D_MODEL), lambda i: (i, 0)),
        out_shape=jax.ShapeDtypeStruct((m, D_MODEL), F32),
        scratch_shapes=[pltpu.VMEM((TM, D_MODEL), F32)],
        compiler_params=_cparams(("parallel",)), name="residual_merge")(x2d, moe_rt)


def _rope_tables(seq):
    t = jnp.arange(seq).astype(F32)
    inv = jnp.power(jnp.float32(ROPE_THETA), -jnp.arange(0, HEAD_DIM, 2, dtype=F32) / HEAD_DIM)
    ang = t[:, None] * inv[None, :]
    cos = jnp.concatenate([jnp.cos(ang), jnp.cos(ang)], axis=1)
    sin = jnp.concatenate([-jnp.sin(ang), jnp.sin(ang)], axis=1)
    return jnp.tile(cos, (1, 4)), jnp.tile(sin, (1, 4))


def _axial_tables(seq):
    t = jnp.arange(seq)
    half = HEAD_DIM // 2
    inv = jnp.power(jnp.float32(ROPE_THETA), -jnp.arange(0, half, 2, dtype=F32) / half)
    ang_r = (t // GRID_W).astype(F32)[:, None] * inv[None, :]
    ang_c = (t % GRID_W).astype(F32)[:, None] * inv[None, :]
    cos = jnp.concatenate([jnp.cos(ang_r), jnp.cos(ang_r), jnp.cos(ang_c), jnp.cos(ang_c)], axis=1)
    sin = jnp.concatenate([-jnp.sin(ang_r), jnp.sin(ang_r), -jnp.sin(ang_c), jnp.sin(ang_c)], axis=1)
    return jnp.tile(cos, (1, 4)), jnp.tile(sin, (1, 4))


def _col_gain(pieces):
    cols = []
    for gain, n_heads, scale in pieces:
        if gain is None:
            cols.append(jnp.ones((n_heads * HEAD_DIM,), F32))
        else:
            cols.append(jnp.tile(gain.astype(F32) * scale, n_heads))
    return jnp.concatenate(cols)[None, :]


def _chunks(start, width, kind, oi):
    out = []
    step = 256 if width % 256 == 0 else LANES
    for c in range(0, width, step):
        out.append((start + c, step, kind, oi, c))
    return out


def _moe_block(x2d, parts, w_out, ffn_gain, w_router, wg, wu, wd, batch, seq):
    cap = EC_CAPACITY_FACTOR * seq // N_EXPERTS
    x_new, h_rt, aff, aff_t = _out_proj(x2d, parts, w_out.astype(BF16), ffn_gain, w_router, batch, seq)
    ids, gates = _topk(aff_t, aff.reshape(batch, seq, LANES), cap)
    ids = ids.reshape(batch * N_EXPERTS, 1, cap)
    gates = gates.reshape(batch * N_EXPERTS, cap, 1)
    moe_rt = _moe(ids, gates, h_rt.reshape(batch, seq * ROW_TILES, LANES),
                  wg.astype(BF16), wu.astype(BF16), wd.astype(BF16), cap)
    return x_new, moe_rt.reshape(batch * seq * ROW_TILES, LANES)


def kernel(x, attn_norm_even, w_in_even, q_norm_a, k_norm_a, rel_bias_a, q_norm_b, k_norm_b, sink_b,
           w_out_even, attn_norm_odd, w_in_odd, q_norm_c, k_norm_c, w_out_odd,
           ffn_norm, w_router, w_gate, w_up, w_down):
    batch, seq, _ = x.shape
    m = batch * seq
    scale = HEAD_DIM ** -0.5
    x2d = x.reshape(m, D_MODEL)
    moe_rt = None
    depth = ffn_norm.shape[0]
    for layer in range(depth):
        i = layer // 2
        if layer % 2 == 0:
            cos_t, sin_t = _rope_tables(seq)
            colgain = _col_gain([(q_norm_a[i], A_HEADS, scale), (k_norm_a[i], A_HEADS, 1.0),
                                 (None, A_HEADS, 1.0), (q_norm_b[i], B_HEADS, scale),
                                 (k_norm_b[i], B_KV_HEADS, 1.0), (None, B_KV_HEADS, 1.0)])
            sections = (_chunks(0, A_W, "norm", 0) + _chunks(A_W, A_W, "norm", 1)
                        + _chunks(2 * A_W, A_W, "plain", 2) + _chunks(3 * A_W, B_QW, "rope", 3)
                        + _chunks(3 * A_W + B_QW, B_KVW, "rope", 4)
                        + _chunks(3 * A_W + B_QW + B_KVW, B_KVW, "plain", 5))
            widths = [A_W, A_W, A_W, B_QW, B_KVW, B_KVW]
            outs = _norm_proj(x2d, moe_rt, attn_norm_even[i], w_in_even[i].astype(BF16), colgain,
                              cos_t, sin_t, sections, widths, HEAD_DIM // 2, seq)
            if moe_rt is not None:
                x2d = outs[-1]
            qa, ka, va, qb, kb, vb = [o.reshape(batch, seq, -1) for o in outs[:6]]
            out_a = _na_attention(qa, ka, va, rel_bias_a[i])
            out_b = _swa_attention(qb, kb, vb, sink_b[i])
            parts = [out_a.reshape(m, A_W), out_b.reshape(m, B_QW)]
            w_out = w_out_even[i]
        else:
            cos_t, sin_t = _axial_tables(seq)
            colgain = _col_gain([(q_norm_c[i], C_HEADS, scale), (k_norm_c[i], C_KV_HEADS, 1.0),
                                 (None, C_KV_HEADS, 1.0)])
            sections = (_chunks(0, C_QW, "rope", 0) + _chunks(C_QW, C_KVW, "rope", 1)
                        + _chunks(C_QW + C_KVW, C_KVW, "plain", 2))
            widths = [C_QW, C_KVW, C_KVW]
            outs = _norm_proj(x2d, moe_rt, attn_norm_odd[i], w_in_odd[i].astype(BF16), colgain,
                              cos_t, sin_t, sections, widths, HEAD_DIM // 4, seq)
            if moe_rt is not None:
                x2d = outs[-1]
            qc, kc, vc = [o.reshape(batch, seq, -1) for o in outs[:3]]
            parts = [_dense_attention(qc, kc, vc).reshape(m, C_QW)]
            w_out = w_out_odd[i]
        x2d, moe_rt = _moe_block(x2d, parts, w_out, ffn_norm[layer], w_router[layer],
                                 w_gate[layer], w_up[layer], w_down[layer], batch, seq)
    return _merge(x2d, moe_rt).reshape(batch, seq, D_MODEL)
```

```python
import functools

import numpy as np
import jax
import jax.numpy as jnp
from jax import lax
from jax.experimental import pallas as pl
from jax.experimental.pallas import tpu as pltpu

D_MODEL = 1024
HEAD_DIM = 64
GRID_W = 64
ROPE_THETA = 10000.0
EPS = 1e-6
NEG_INF = -1e30
LOG2E = 1.4426950408889634

A_HEADS = 8
NA_WIN_ROWS = 8
NA_WIN_COLS = 16
B_HEADS = 8
B_KV_HEADS = 2
B_WINDOW = 128
C_HEADS = 16
C_KV_HEADS = 4
N_EXPERTS = 16
EC_CAPACITY_FACTOR = 2
D_FF = 2048

A_W = A_HEADS * HEAD_DIM
B_QW = B_HEADS * HEAD_DIM
B_KVW = B_KV_HEADS * HEAD_DIM
C_QW = C_HEADS * HEAD_DIM
C_KVW = C_KV_HEADS * HEAD_DIM

SUBLANES = 8
LANES = 128
ROW_TILES = D_MODEL // LANES
VMEM_LIMIT = 56 * 1024 * 1024

TM = 512
NA_QROWS = 4
NA_KROWS = 12
SWA_Q = 256
SWA_K = 512
DENSE_Q = 256
DENSE_UNROLL = 4

F32 = jnp.float32
BF16 = jnp.bfloat16


def _cparams(sem):
    return pltpu.CompilerParams(dimension_semantics=sem, vmem_limit_bytes=VMEM_LIMIT)


def _to_rowtile(src_ref, dst_ref, rows):
    def body(i, carry):
        r0 = pl.multiple_of(i * SUBLANES, SUBLANES)
        base = i * (SUBLANES * ROW_TILES)
        for c in range(ROW_TILES):
            dst_ref[pl.ds(base + c, SUBLANES, stride=ROW_TILES), :] = (
                src_ref[pl.ds(r0, SUBLANES), c * LANES:(c + 1) * LANES])
        return carry
    lax.fori_loop(0, rows // SUBLANES, body, 0)


def _from_rowtile(src_ref, dst_ref, rows):
    def body(i, carry):
        r0 = pl.multiple_of(i * SUBLANES, SUBLANES)
        base = i * (SUBLANES * ROW_TILES)
        for c in range(ROW_TILES):
            dst_ref[pl.ds(r0, SUBLANES), c * LANES:(c + 1) * LANES] = (
                src_ref[pl.ds(base + c, SUBLANES, stride=ROW_TILES), :])
        return carry
    lax.fori_loop(0, rows // SUBLANES, body, 0)


def _rms_rows(x):
    return x * lax.rsqrt(jnp.mean(x * x, axis=-1, keepdims=True) + EPS)


def _norm_proj_kernel(*refs, sections, has_moe, half):
    it = iter(refs)
    x_ref = next(it)
    moe_ref = next(it) if has_moe else None
    gain_ref = next(it)
    w_ref = next(it)
    cg_ref = next(it)
    cos_ref = next(it)
    sin_ref = next(it)
    bd_ref = next(it)
    n_out = len({s[3] for s in sections})
    outs = [next(it) for _ in range(n_out)]
    xo_ref = next(it) if has_moe else None
    xs_ref = next(it) if has_moe else None

    if has_moe:
        _from_rowtile(moe_ref, xs_ref, TM)
        x = x_ref[...] + xs_ref[...]
        xo_ref[...] = x
    else:
        x = x_ref[...]
    hb = (_rms_rows(x) * gain_ref[...]).astype(BF16)

    ys = [jnp.dot(hb, w_ref[:, c0:c0 + width], preferred_element_type=F32)
          for (c0, width, _, _, _) in sections]
    sss = []
    for y, (c0, width, kind, _, _) in zip(ys, sections):
        if kind == "plain":
            sss.append(None)
            continue
        y2 = y * y
        hi = y2.astype(BF16)
        lo = (y2 - hi.astype(F32)).astype(BF16)
        bd = bd_ref[:width, :width]
        sss.append(jnp.dot(hi, bd, preferred_element_type=F32)
                   + jnp.dot(lo, bd, preferred_element_type=F32))
    for y, ss, (c0, width, kind, oi, oc) in zip(ys, sss, sections):
        if kind != "plain":
            y = y * lax.rsqrt(ss + EPS) * cg_ref[:, c0:c0 + width]
        if kind == "rope":
            lane = lax.broadcasted_iota(jnp.int32, (1, width), 1)
            first = (lane % (2 * half)) < half
            part = jnp.where(first, pltpu.roll(y, width - half, 1), pltpu.roll(y, half, 1))
            y = y * cos_ref[:, :width] + part * sin_ref[:, :width]
        outs[oi][:, oc:oc + width] = y.astype(BF16)


def _norm_proj(x2d, moe_rt, gain, w_bf16, colgain, cos_t, sin_t, sections, out_widths, half, seq):
    m = x2d.shape[0]
    n_in = w_bf16.shape[1]
    has_moe = moe_rt is not None
    bd = jnp.asarray(np.kron(np.eye(4), np.full((HEAD_DIM, HEAD_DIM), 1.0 / HEAD_DIM)), BF16)
    tiles_per_seq = seq // TM
    in_specs = [pl.BlockSpec((TM, D_MODEL), lambda i: (i, 0))]
    args = [x2d]
    if has_moe:
        in_specs.append(pl.BlockSpec((TM * ROW_TILES, LANES), lambda i: (i, 0)))
        args.append(moe_rt)
    in_specs += [
        pl.BlockSpec((1, D_MODEL), lambda i: (0, 0)),
        pl.BlockSpec((D_MODEL, n_in), lambda i: (0, 0)),
        pl.BlockSpec((1, n_in), lambda i: (0, 0)),
        pl.BlockSpec((TM, 256), lambda i: (i % tiles_per_seq, 0)),
        pl.BlockSpec((TM, 256), lambda i: (i % tiles_per_seq, 0)),
        pl.BlockSpec((256, 256), lambda i: (0, 0)),
    ]
    args += [gain.reshape(1, D_MODEL), w_bf16, colgain, cos_t, sin_t, bd]
    out_shape = [jax.ShapeDtypeStruct((m, wd), BF16) for wd in out_widths]
    out_specs = [pl.BlockSpec((TM, wd), lambda i: (i, 0)) for wd in out_widths]
    scratch = []
    if has_moe:
        out_shape.append(jax.ShapeDtypeStruct((m, D_MODEL), F32))
        out_specs.append(pl.BlockSpec((TM, D_MODEL), lambda i: (i, 0)))
        scratch.append(pltpu.VMEM((TM, D_MODEL), F32))
    kern = functools.partial(_norm_proj_kernel, sections=tuple(sections), has_moe=has_moe, half=half)
    return pl.pallas_call(
        kern, grid=(m // TM,), in_specs=in_specs, out_specs=out_specs, out_shape=out_shape,
        scratch_shapes=scratch, compiler_params=_cparams(("parallel",)),
        name="norm_proj_moe" if has_moe else "norm_proj")(*args)


def _na_kernel(q_ref, k_ref, v_ref, bias_ref, o_ref):
    nq = NA_QROWS * GRID_W
    nk = NA_KROWS * GRID_W
    rows = q_ref.shape[1] // GRID_W
    n_blocks = rows // NA_QROWS
    lane = lax.broadcasted_iota(jnp.int32, (1, LANES), 1)
    head0 = lane < HEAD_DIM

    for qb in range(n_blocks):
        r0 = qb * NA_QROWS
        start = min(max(r0 - NA_WIN_ROWS // 2, 0), rows - NA_KROWS)
        pat = 0 if qb == 0 else (2 if qb == n_blocks - 1 else 1)
        qs = r0 * GRID_W
        ks = start * GRID_W
        q = q_ref[0, qs:qs + nq, :]
        k = k_ref[0, ks:ks + nk, :]
        v = v_ref[0, ks:ks + nk, :]
        acc = None
        for hh in range(2):
            sel = head0 if hh == 0 else jnp.logical_not(head0)
            qm = jnp.where(sel, q, jnp.zeros_like(q))
            s = lax.dot_general(qm, k, (((1,), (1,)), ((), ())), preferred_element_type=F32)
            s = s + bias_ref[pat, hh]
            m = jnp.max(s, axis=-1, keepdims=True)
            p = jnp.exp2(s - m)
            l = jnp.sum(p, axis=-1, keepdims=True)
            o = jnp.dot(p.astype(BF16), v, preferred_element_type=F32) / l
            acc = o if acc is None else jnp.where(sel, o, acc)
        o_ref[0, qs:qs + nq, :] = acc.astype(BF16)


def _na_bias_tables(rel_bias, rows):
    n_dr, n_dc = 2 * NA_WIN_ROWS - 1, 2 * NA_WIN_COLS - 1
    sel_r = np.zeros((3, NA_QROWS, NA_KROWS, n_dr), np.float32)
    in_r = np.zeros((3, NA_QROWS, NA_KROWS), np.float32)
    for p, r0 in enumerate((0, 2 * NA_QROWS, rows - NA_QROWS)):
        start = int(np.clip(r0 - NA_WIN_ROWS // 2, 0, rows - NA_KROWS))
        for lr in range(NA_QROWS):
            r = r0 + lr
            rs = int(np.clip(r - NA_WIN_ROWS // 2, 0, rows - NA_WIN_ROWS))
            for kl in range(NA_KROWS):
                kr = start + kl
                if rs <= kr < rs + NA_WIN_ROWS:
                    in_r[p, lr, kl] = 1.0
                    sel_r[p, lr, kl, kr - r + NA_WIN_ROWS - 1] = 1.0
    sel_c = np.zeros((GRID_W, GRID_W, n_dc), np.float32)
    in_c = np.zeros((GRID_W, GRID_W), np.float32)
    for c in range(GRID_W):
        cs = int(np.clip(c - NA_WIN_COLS // 2, 0, GRID_W - NA_WIN_COLS))
        for kc in range(cs, cs + NA_WIN_COLS):
            in_c[c, kc] = 1.0
            dc = int(np.clip(kc - c, -(NA_WIN_COLS - 1), NA_WIN_COLS - 1))
            sel_c[c, kc, dc + NA_WIN_COLS - 1] = 1.0
    hp = lax.Precision.HIGHEST
    rows_sel = jnp.einsum('plkd,hde->phlke', jnp.asarray(sel_r), rel_bias.astype(F32), precision=hp)
    vals = jnp.einsum('phlke,cje->phlckj', rows_sel, jnp.asarray(sel_c), precision=hp)
    inside = jnp.asarray(in_r[:, None, :, None, :, None] * in_c[None, None, None, :, None, :])
    tab = jnp.where(inside > 0.5, vals * LOG2E, NEG_INF)
    return tab.reshape(3, rel_bias.shape[0], NA_QROWS * GRID_W, NA_KROWS * GRID_W)


def _na_attention(q, k, v, rel_bias):
    b, s, _ = q.shape
    rows = s // GRID_W
    assert rows % NA_QROWS == 0 and rows >= NA_KROWS and rows // NA_QROWS >= 3
    nq, nk = NA_QROWS * GRID_W, NA_KROWS * GRID_W
    bias = _na_bias_tables(rel_bias, rows)
    blk = pl.BlockSpec((1, s, LANES), lambda bi, hp: (bi, 0, hp))
    return pl.pallas_call(
        _na_kernel, grid=(b, A_HEADS // 2),
        in_specs=[blk, blk, blk, pl.BlockSpec((3, 2, nq, nk), lambda bi, hp: (0, hp, 0, 0))],
        out_specs=blk, out_shape=jax.ShapeDtypeStruct((b, s, A_W), BF16),
        compiler_params=_cparams(("parallel", "arbitrary")), name="na_attention")(q, k, v, bias)


def _swa_kernel(sink_ref, q_ref, k_ref, v_ref, rep_ref, o_ref, kt_ref, vt_ref):
    s_len = q_ref.shape[1]
    n_groups = B_HEADS // B_KV_HEADS
    gw = n_groups * HEAD_DIM
    lane = lax.broadcasted_iota(jnp.int32, (1, gw), 1)
    rel = (lax.broadcasted_iota(jnp.int32, (SWA_Q, SWA_K), 0)
           - lax.broadcasted_iota(jnp.int32, (SWA_Q, SWA_K), 1))

    for g in range(B_KV_HEADS):
        kt_ref[...] = jnp.dot(k_ref[0], rep_ref[g], preferred_element_type=F32).astype(BF16)
        vt_ref[...] = jnp.dot(v_ref[0], rep_ref[g], preferred_element_type=F32).astype(BF16)

        for i in range(s_len // SWA_Q):
            q0 = i * SWA_Q
            k0 = min(max(q0 - B_WINDOW, 0), s_len - SWA_K)
            q = q_ref[0, q0:q0 + SWA_Q, g * gw:(g + 1) * gw]
            kt = kt_ref[k0:k0 + SWA_K, :]
            vt = vt_ref[k0:k0 + SWA_K, :]
            valid = jnp.abs(rel + (q0 - k0)) <= B_WINDOW
            acc = jnp.zeros((SWA_Q, gw), F32)
            for j in range(n_groups):
                sel = (lane >= j * HEAD_DIM) & (lane < (j + 1) * HEAD_DIM)
                sink = sink_ref[g * n_groups + j]
                qm = jnp.where(sel, q, jnp.zeros_like(q))
                s = lax.dot_general(qm, kt, (((1,), (1,)), ((), ())), preferred_element_type=F32)
                s = jnp.where(valid, s, NEG_INF)
                m = jnp.maximum(jnp.max(s, axis=-1, keepdims=True), sink)
                e = jnp.exp2(s - m)
                den = jnp.sum(e, axis=-1, keepdims=True) + jnp.exp2(sink - m)
                o = jnp.dot(e.astype(BF16), vt, preferred_element_type=F32) / den
                acc = jnp.where(sel, o, acc)
            o_ref[0, q0:q0 + SWA_Q, g * gw:(g + 1) * gw] = acc.astype(BF16)


def _replication_matrices(n_kv):
    rep = np.zeros((n_kv, n_kv * HEAD_DIM, 4 * HEAD_DIM), np.float32)
    for g in range(n_kv):
        for j in range(4 * HEAD_DIM):
            rep[g, g * HEAD_DIM + j % HEAD_DIM, j] = 1.0
    return jnp.asarray(rep, BF16)


def _swa_attention(q, k, v, sink):
    b, s, _ = q.shape
    assert s % SWA_Q == 0 and s >= SWA_K and SWA_K >= SWA_Q + 2 * B_WINDOW
    rep = _replication_matrices(B_KV_HEADS)
    return pl.pallas_call(
        _swa_kernel, grid=(b,),
        in_specs=[pl.BlockSpec(memory_space=pltpu.SMEM),
                  pl.BlockSpec((1, s, B_QW), lambda bi: (bi, 0, 0)),
                  pl.BlockSpec((1, s, B_KVW), lambda bi: (bi, 0, 0)),
                  pl.BlockSpec((1, s, B_KVW), lambda bi: (bi, 0, 0)),
                  pl.BlockSpec((B_KV_HEADS, B_KVW, 4 * HEAD_DIM), lambda bi: (0, 0, 0))],
        out_specs=pl.BlockSpec((1, s, B_QW), lambda bi: (bi, 0, 0)),
        out_shape=jax.ShapeDtypeStruct((b, s, B_QW), BF16),
        scratch_shapes=[pltpu.VMEM((s, 4 * HEAD_DIM), BF16), pltpu.VMEM((s, 4 * HEAD_DIM), BF16)],
        compiler_params=_cparams(("parallel",)), name="swa_attention")(sink.astype(F32) * LOG2E, q, k, v, rep)


def _dense_kernel(q_ref, k_ref, v_ref, rep_ref, o_ref, kt_ref, vt_ref):
    s_len = q_ref.shape[1]
    gw = 4 * HEAD_DIM
    lane = lax.broadcasted_iota(jnp.int32, (1, gw), 1)
    kt_ref[...] = jnp.dot(k_ref[0], rep_ref[0], preferred_element_type=F32).astype(BF16)
    vt_ref[...] = jnp.dot(v_ref[0], rep_ref[0], preferred_element_type=F32).astype(BF16)

    def body(i, carry):
        for u in range(DENSE_UNROLL):
            q0 = pl.multiple_of((i * DENSE_UNROLL + u) * DENSE_Q, DENSE_Q)
            q = q_ref[0, pl.ds(q0, DENSE_Q), :]
            acc = jnp.zeros((DENSE_Q, gw), F32)
            for j in range(4):
                sel = (lane >= j * HEAD_DIM) & (lane < (j + 1) * HEAD_DIM)
                qm = jnp.where(sel, q, jnp.zeros_like(q))
                s = lax.dot_general(qm, kt_ref[...], (((1,), (1,)), ((), ())), preferred_element_type=F32)
                m = jnp.max(s, axis=-1, keepdims=True)
                p = jnp.exp2(s - m)
                l = jnp.sum(p, axis=-1, keepdims=True)
                o = jnp.dot(p.astype(BF16), vt_ref[...], preferred_element_type=F32) / l
                acc = jnp.where(sel, o, acc)
            o_ref[0, pl.ds(q0, DENSE_Q), :] = acc.astype(BF16)
        return carry

    lax.fori_loop(0, s_len // (DENSE_Q * DENSE_UNROLL), body, 0)


def _dense_attention(q, k, v):
    b, s, _ = q.shape
    gw = 4 * HEAD_DIM
    rep = _replication_matrices(C_KV_HEADS)
    return pl.pallas_call(
        _dense_kernel, grid=(b, C_KV_HEADS),
        in_specs=[pl.BlockSpec((1, s, gw), lambda bi, g: (bi, 0, g)),
                  pl.BlockSpec((1, s, C_KVW), lambda bi, g: (bi, 0, 0)),
                  pl.BlockSpec((1, s, C_KVW), lambda bi, g: (bi, 0, 0)),
                  pl.BlockSpec((1, C_KVW, gw), lambda bi, g: (g, 0, 0))],
        out_specs=pl.BlockSpec((1, s, gw), lambda bi, g: (bi, 0, g)),
        out_shape=jax.ShapeDtypeStruct((b, s, C_QW), BF16),
        scratch_shapes=[pltpu.VMEM((s, gw), BF16), pltpu.VMEM((s, gw), BF16)],
        compiler_params=_cparams(("parallel", "arbitrary")), name="dense_attention")(q, k, v, rep)


def _out_proj_kernel(*refs, n_parts):
    it = iter(refs)
    x_ref = next(it)
    parts = [next(it) for _ in range(n_parts)]
    w_ref = next(it)
    gain_ref = next(it)
    wr_ref = next(it)
    xo_ref = next(it)
    hrt_ref = next(it)
    aff_ref = next(it)
    afft_ref = next(it)
    h_ref = next(it)

    y = x_ref[...]
    c0 = 0
    for p_ref in parts:
        wp = p_ref.shape[1]
        y = y + jnp.dot(p_ref[...], w_ref[c0:c0 + wp, :], preferred_element_type=F32)
        c0 += wp
    xo_ref[...] = y
    h = _rms_rows(y) * gain_ref[...]
    h_ref[...] = h
    _to_rowtile(h_ref, hrt_ref, TM)

    h_hi = h.astype(BF16)
    h_lo = (h - h_hi.astype(F32)).astype(BF16)
    r_hi = jnp.dot(h_hi, wr_ref[...], preferred_element_type=F32)
    r_lo = jnp.dot(h_lo, wr_ref[:, :LANES], preferred_element_type=F32)
    lane = lax.broadcasted_iota(jnp.int32, (1, LANES), 1)
    logits = r_hi[:, :LANES] + r_hi[:, LANES:] + r_lo
    logits = jnp.where(lane < N_EXPERTS, logits, NEG_INF)
    e = jnp.exp(logits - jnp.max(logits, axis=-1, keepdims=True))
    aff = e / jnp.sum(e, axis=-1, keepdims=True)
    aff_ref[...] = aff
    afft_ref[0] = jnp.transpose(aff)[:N_EXPERTS, :]


def _out_proj(x2d, parts, w_out_bf16, ffn_gain, w_router, batch, seq):
    m = x2d.shape[0]
    tiles_per_seq = seq // TM
    wr = w_router.astype(F32)
    wr_hi = wr.astype(BF16)
    wr_lo = (wr - wr_hi.astype(F32)).astype(BF16)
    wr_split = jnp.zeros((D_MODEL, 2 * LANES), BF16)
    wr_split = wr_split.at[:, :N_EXPERTS].set(wr_hi).at[:, LANES:LANES + N_EXPERTS].set(wr_lo)
    in_specs = [pl.BlockSpec((TM, D_MODEL), lambda i: (i, 0))]
    in_specs += [pl.BlockSpec((TM, p.shape[1]), lambda i: (i, 0)) for p in parts]
    in_specs += [pl.BlockSpec((D_MODEL, D_MODEL), lambda i: (0, 0)),
                 pl.BlockSpec((1, D_MODEL), lambda i: (0, 0)),
                 pl.BlockSpec((D_MODEL, 2 * LANES), lambda i: (0, 0))]
    out_shape = [jax.ShapeDtypeStruct((m, D_MODEL), F32),
                 jax.ShapeDtypeStruct((m * ROW_TILES, LANES), F32),
                 jax.ShapeDtypeStruct((m, LANES), F32),
                 jax.ShapeDtypeStruct((batch, N_EXPERTS, seq), F32)]
    out_specs = [pl.BlockSpec((TM, D_MODEL), lambda i: (i, 0)),
                 pl.BlockSpec((TM * ROW_TILES, LANES), lambda i: (i, 0)),
                 pl.BlockSpec((TM, LANES), lambda i: (i, 0)),
                 pl.BlockSpec((1, N_EXPERTS, TM), lambda i: (i // tiles_per_seq, 0, i % tiles_per_seq))]
    kern = functools.partial(_out_proj_kernel, n_parts=len(parts))
    return pl.pallas_call(
        kern, grid=(m // TM,), in_specs=in_specs, out_specs=out_specs, out_shape=out_shape,
        scratch_shapes=[pltpu.VMEM((TM, D_MODEL), F32)],
        compiler_params=_cparams(("parallel",)), name="out_proj_router")(
            x2d, *parts, w_out_bf16, ffn_gain.reshape(1, D_MODEL), wr_split)


def _topk_kernel(afft_ref, aff_ref, ids_ref, gate_ref, *, cap):
    s_len = afft_ref.shape[2]
    at = afft_ref[0]
    bits = pltpu.bitcast(at, jnp.int32)

    def count(mask):
        return jnp.sum(jnp.where(mask, 1.0, 0.0), axis=1, keepdims=True)

    def bit_step(k, thr):
        cand = thr | jnp.left_shift(jnp.int32(1), 30 - k)
        return jnp.where(count(bits >= cand) >= cap, cand, thr)

    thr = lax.fori_loop(0, 31, bit_step, jnp.zeros((N_EXPERTS, 1), jnp.int32))
    gt = bits > thr
    eq = bits == thr
    need = cap - count(gt)

    ri = lax.broadcasted_iota(jnp.int32, (LANES, LANES), 0)
    ci = lax.broadcasted_iota(jnp.int32, (LANES, LANES), 1)
    upper = jnp.where(ri < ci, 1.0, 0.0).astype(BF16)

    def prefix(mask_f32):
        out = []
        run = jnp.zeros((N_EXPERTS, 1), F32)
        for c in range(s_len // LANES):
            blk = mask_f32[:, c * LANES:(c + 1) * LANES]
            out.append(jnp.dot(blk.astype(BF16), upper, preferred_element_type=F32) + run)
            run = run + jnp.sum(blk, axis=1, keepdims=True)
        return out

    eq_f = jnp.where(eq, 1.0, 0.0)
    eq_rank = prefix(eq_f)
    sel_tiles = []
    for c in range(s_len // LANES):
        sl = slice(c * LANES, (c + 1) * LANES)
        sel_tiles.append(jnp.where(gt[:, sl] | (eq[:, sl] & (eq_rank[c] < need)), 1.0, 0.0))
    sel_f = jnp.concatenate(sel_tiles, axis=1)
    pos = jnp.concatenate(prefix(sel_f), axis=1)
    key = jnp.where(sel_f > 0.5, pos, -1.0)

    a = aff_ref[0]
    a_hi = a.astype(BF16).astype(F32)
    r1 = a - a_hi
    a_mid = r1.astype(BF16).astype(F32)
    a_lo = r1 - a_mid
    lane = lax.broadcasted_iota(jnp.int32, (s_len, LANES), 1)
    tok = lax.broadcasted_iota(jnp.int32, (s_len, LANES), 0)
    packed = a_hi + pltpu.roll(a_mid, N_EXPERTS, 1) + pltpu.roll(a_lo, 2 * N_EXPERTS, 1)
    packed = jnp.where(lane == 64, (tok // 64).astype(F32), packed)
    packed = jnp.where(lane == 65, (tok % 64).astype(F32), packed)
    rmat = packed.astype(BF16)

    slot = lax.broadcasted_iota(jnp.int32, (cap, 1), 0).astype(F32)
    for e in range(N_EXPERTS):
        onehot_t = jnp.where(key[e:e + 1, :] == slot, 1.0, 0.0).astype(BF16)
        res = jnp.dot(onehot_t, rmat, preferred_element_type=F32)
        ids_ref[0, e] = (res[:, 64:65] * 64.0 + res[:, 65:66]).astype(jnp.int32)
        gate_ref[0, e] = (res[:, e:e + 1] + res[:, N_EXPERTS + e:N_EXPERTS + e + 1]
                          + res[:, 2 * N_EXPERTS + e:2 * N_EXPERTS + e + 1])


def _topk(aff_t, aff, cap):
    b, _, s = aff_t.shape
    kern = functools.partial(_topk_kernel, cap=cap)
    return pl.pallas_call(
        kern, grid=(b,),
        in_specs=[pl.BlockSpec((1, N_EXPERTS, s), lambda bi: (bi, 0, 0)),
                  pl.BlockSpec((1, s, LANES), lambda bi: (bi, 0, 0))],
        out_specs=[pl.BlockSpec((1, N_EXPERTS, cap, 1), lambda bi: (bi, 0, 0, 0)),
                   pl.BlockSpec((1, N_EXPERTS, cap, 1), lambda bi: (bi, 0, 0, 0))],
        out_shape=[jax.ShapeDtypeStruct((b, N_EXPERTS, cap, 1), jnp.int32),
                   jax.ShapeDtypeStruct((b, N_EXPERTS, cap, 1), F32)],
        compiler_params=_cparams(("parallel",)), name="expert_topk")(aff_t, aff)


MOE_GROUP = 8


MOE_EXPERTS_PER_STEP = 4
FFN_ROWS = 512
FFN_F_CHUNK = 512


def _gather_kernel(ids_ref, h_ref, o_ref, *, cap):
    for el in range(MOE_EXPERTS_PER_STEP):
        def body(jg, carry, el=el):
            for u in range(MOE_GROUP):
                j = jg * MOE_GROUP + u
                src = pl.multiple_of(ids_ref[0, 0, el * cap + j] * ROW_TILES, ROW_TILES)
                dst = pl.multiple_of(j * ROW_TILES, ROW_TILES)
                o_ref[el, pl.ds(dst, ROW_TILES), :] = h_ref[0, pl.ds(src, ROW_TILES), :]
            return carry
        lax.fori_loop(0, cap // MOE_GROUP, body, 0)


def _ffn_kernel(xe_ref, gate_ref, wg_ref, wu_ref, wd_ref, o_ref, xs_ref):
    _from_rowtile(xe_ref.at[0], xs_ref, FFN_ROWS)
    xe = xs_ref[...].astype(BF16)
    y = None
    for f0 in range(0, D_FF, FFN_F_CHUNK):
        hg = jnp.dot(xe, wg_ref[0, :, f0:f0 + FFN_F_CHUNK], preferred_element_type=F32)
        hu = jnp.dot(xe, wu_ref[0, :, f0:f0 + FFN_F_CHUNK], preferred_element_type=F32)
        hid = (hg * (1.0 / (1.0 + jnp.exp(-hg))) * hu).astype(BF16)
        part = jnp.dot(hid, wd_ref[0, f0:f0 + FFN_F_CHUNK, :], preferred_element_type=F32)
        y = part if y is None else y + part
    xs_ref[...] = y * gate_ref[0]
    _to_rowtile(xs_ref, o_ref.at[0], FFN_ROWS)


def _scatter_kernel(ids_ref, y_ref, o_ref, *, cap):
    @pl.when(pl.program_id(1) == 0)
    def _():
        o_ref[...] = jnp.zeros_like(o_ref)

    for el in range(MOE_EXPERTS_PER_STEP):
        def body(jg, carry, el=el):
            rows = []
            for u in range(MOE_GROUP):
                j = jg * MOE_GROUP + u
                rows.append(pl.multiple_of(ids_ref[0, 0, el * cap + j] * ROW_TILES, ROW_TILES))
            vals = []
            for u in range(MOE_GROUP):
                src = pl.multiple_of((jg * MOE_GROUP + u) * ROW_TILES, ROW_TILES)
                vals.append(o_ref[0, pl.ds(rows[u], ROW_TILES), :] + y_ref[el, pl.ds(src, ROW_TILES), :])
            for u in range(MOE_GROUP):
                o_ref[0, pl.ds(rows[u], ROW_TILES), :] = vals[u]
            return carry
        lax.fori_loop(0, cap // MOE_GROUP, body, 0)


def _moe(ids, gates, h_rt, wg, wu, wd, cap):
    b, rt_rows, _ = h_rt.shape
    n_eg = N_EXPERTS // MOE_EXPERTS_PER_STEP
    step_ids = MOE_EXPERTS_PER_STEP * cap
    ids_g = ids.reshape(b * n_eg, 1, step_ids)
    slot_rows = cap * ROW_TILES
    ids_spec = pl.BlockSpec((1, 1, step_ids), lambda bi, eg: (bi * n_eg + eg, 0, 0), memory_space=pltpu.SMEM)
    seq_spec = pl.BlockSpec((1, rt_rows, LANES), lambda bi, eg: (bi, 0, 0))
    slot_spec = pl.BlockSpec((MOE_EXPERTS_PER_STEP, slot_rows, LANES), lambda bi, eg: (eg, bi, 0))
    slots = jax.ShapeDtypeStruct((N_EXPERTS, b * slot_rows, LANES), F32)

    xe_rt = pl.pallas_call(
        functools.partial(_gather_kernel, cap=cap), grid=(b, n_eg),
        in_specs=[ids_spec, seq_spec], out_specs=slot_spec, out_shape=slots,
        compiler_params=_cparams(("parallel", "arbitrary")), name="moe_gather")(ids_g, h_rt)

    assert (b * cap) % FFN_ROWS == 0
    ffn_rt = FFN_ROWS * ROW_TILES
    ye_rt = pl.pallas_call(
        _ffn_kernel, grid=(N_EXPERTS, b * cap // FFN_ROWS),
        in_specs=[pl.BlockSpec((1, ffn_rt, LANES), lambda e, i: (e, i, 0)),
                  pl.BlockSpec((1, FFN_ROWS, 1), lambda e, i: (e, i, 0)),
                  pl.BlockSpec((1, D_MODEL, D_FF), lambda e, i: (e, 0, 0)),
                  pl.BlockSpec((1, D_MODEL, D_FF), lambda e, i: (e, 0, 0)),
                  pl.BlockSpec((1, D_FF, D_MODEL), lambda e, i: (e, 0, 0))],
        out_specs=pl.BlockSpec((1, ffn_rt, LANES), lambda e, i: (e, i, 0)), out_shape=slots,
        scratch_shapes=[pltpu.VMEM((FFN_ROWS, D_MODEL), F32)],
        compiler_params=_cparams(("parallel", "arbitrary")), name="moe_ffn")(xe_rt, gates, wg, wu, wd)

    return pl.pallas_call(
        functools.partial(_scatter_kernel, cap=cap), grid=(b, n_eg),
        in_specs=[ids_spec, slot_spec], out_specs=seq_spec,
        out_shape=jax.ShapeDtypeStruct((b, rt_rows, LANES), F32),
        compiler_params=_cparams(("parallel", "arbitrary")), name="moe_scatter")(ids_g, ye_rt)


def _merge_kernel(x_ref, moe_ref, o_ref, xs_ref):
    _from_rowtile(moe_ref, xs_ref, TM)
    o_ref[...] = x_ref[...] + xs_ref[...]


def _merge(x2d, moe_rt):
    m = x2d.shape[0]
    return pl.pallas_call(
        _merge_kernel, grid=(m // TM,),
        in_specs=[pl.BlockSpec((TM, D_MODEL), lambda i: (i, 0)),
                  pl.BlockSpec((TM * ROW_TILES, LANES), lambda i: (i, 0))],
        out_specs=pl.BlockSpec((TM, D_MODEL), lambda i: (i, 0)),
        out_shape=jax.ShapeDtypeStruct((m, D_MODEL), F32),
        scratch_shapes=[pltpu.VMEM((TM, D_MODEL), F32)],
        compiler_params=_cparams(("parallel",)), name="residual_merge")(x2d, moe_rt)


def _rope_tables(seq):
    t = jnp.arange(seq).astype(F32)
    inv = jnp.power(jnp.float32(ROPE_THETA), -jnp.arange(0, HEAD_DIM, 2, dtype=F32) / HEAD_DIM)
    ang = t[:, None] * inv[None, :]
    cos = jnp.concatenate([jnp.cos(ang), jnp.cos(ang)], axis=1)
    sin = jnp.concatenate([-jnp.sin(ang), jnp.sin(ang)], axis=1)
    return jnp.tile(cos, (1, 4)), jnp.tile(sin, (1, 4))


def _axial_tables(seq):
    t = jnp.arange(seq)
    half = HEAD_DIM // 2
    inv = jnp.power(jnp.float32(ROPE_THETA), -jnp.arange(0, half, 2, dtype=F32) / half)
    ang_r = (t // GRID_W).astype(F32)[:, None] * inv[None, :]
    ang_c = (t % GRID_W).astype(F32)[:, None] * inv[None, :]
    cos = jnp.concatenate([jnp.cos(ang_r), jnp.cos(ang_r), jnp.cos(ang_c), jnp.cos(ang_c)], axis=1)
    sin = jnp.concatenate([-jnp.sin(ang_r), jnp.sin(ang_r), -jnp.sin(ang_c), jnp.sin(ang_c)], axis=1)
    return jnp.tile(cos, (1, 4)), jnp.tile(sin, (1, 4))


def _col_gain(pieces):
    cols = []
    for gain, n_heads, scale in pieces:
        if gain is None:
            cols.append(jnp.ones((n_heads * HEAD_DIM,), F32))
        else:
            cols.append(jnp.tile(gain.astype(F32) * scale, n_heads))
    return jnp.concatenate(cols)[None, :]


def _chunks(start, width, kind, oi):
    out = []
    step = 256 if width % 256 == 0 else LANES
    for c in range(0, width, step):
        out.append((start + c, step, kind, oi, c))
    return out


def _moe_block(x2d, parts, w_out, ffn_gain, w_router, wg, wu, wd, batch, seq):
    cap = EC_CAPACITY_FACTOR * seq // N_EXPERTS
    x_new, h_rt, aff, aff_t = _out_proj(x2d, parts, w_out.astype(BF16), ffn_gain, w_router, batch, seq)
    ids, gates = _topk(aff_t, aff.reshape(batch, seq, LANES), cap)
    ids = ids.reshape(batch, N_EXPERTS, cap)
    gates = jnp.transpose(gates.reshape(batch, N_EXPERTS, cap), (1, 0, 2)).reshape(N_EXPERTS, batch * cap, 1)
    moe_rt = _moe(ids, gates, h_rt.reshape(batch, seq * ROW_TILES, LANES),
                  wg.astype(BF16), wu.astype(BF16), wd.astype(BF16), cap)
    return x_new, moe_rt.reshape(batch * seq * ROW_TILES, LANES)


def kernel(x, attn_norm_even, w_in_even, q_norm_a, k_norm_a, rel_bias_a, q_norm_b, k_norm_b, sink_b,
           w_out_even, attn_norm_odd, w_in_odd, q_norm_c, k_norm_c, w_out_odd,
           ffn_norm, w_router, w_gate, w_up, w_down):
    batch, seq, _ = x.shape
    m = batch * seq
    scale = HEAD_DIM ** -0.5 * LOG2E
    x2d = x.reshape(m, D_MODEL)
    moe_rt = None
    depth = ffn_norm.shape[0]
    for layer in range(depth):
        i = layer // 2
        if layer % 2 == 0:
            cos_t, sin_t = _rope_tables(seq)
            colgain = _col_gain([(q_norm_a[i], A_HEADS, scale), (k_norm_a[i], A_HEADS, 1.0),
                                 (None, A_HEADS, 1.0), (q_norm_b[i], B_HEADS, scale),
                                 (k_norm_b[i], B_KV_HEADS, 1.0), (None, B_KV_HEADS, 1.0)])
            sections = (_chunks(0, A_W, "norm", 0) + _chunks(A_W, A_W, "norm", 1)
                        + _chunks(2 * A_W, A_W, "plain", 2) + _chunks(3 * A_W, B_QW, "rope", 3)
                        + _chunks(3 * A_W + B_QW, B_KVW, "rope", 4)
                        + _chunks(3 * A_W + B_QW + B_KVW, B_KVW, "plain", 5))
            widths = [A_W, A_W, A_W, B_QW, B_KVW, B_KVW]
            outs = _norm_proj(x2d, moe_rt, attn_norm_even[i], w_in_even[i].astype(BF16), colgain,
                              cos_t, sin_t, sections, widths, HEAD_DIM // 2, seq)
            if moe_rt is not None:
                x2d = outs[-1]
            qa, ka, va, qb, kb, vb = [o.reshape(batch, seq, -1) for o in outs[:6]]
            out_a = _na_attention(qa, ka, va, rel_bias_a[i])
            out_b = _swa_attention(qb, kb, vb, sink_b[i])
            parts = [out_a.reshape(m, A_W), out_b.reshape(m, B_QW)]
            w_out = w_out_even[i]
        else:
            cos_t, sin_t = _axial_tables(seq)
            colgain = _col_gain([(q_norm_c[i], C_HEADS, scale), (k_norm_c[i], C_KV_HEADS, 1.0),
                                 (None, C_KV_HEADS, 1.0)])
            sections = (_chunks(0, C_QW, "rope", 0) + _chunks(C_QW, C_KVW, "rope", 1)
                        + _chunks(C_QW + C_KVW, C_KVW, "plain", 2))
            widths = [C_QW, C_KVW, C_KVW]
            outs = _norm_proj(x2d, moe_rt, attn_norm_odd[i], w_in_odd[i].astype(BF16), colgain,
                              cos_t, sin_t, sections, widths, HEAD_DIM // 4, seq)
            if moe_rt is not None:
                x2d = outs[-1]
            qc, kc, vc = [o.reshape(batch, seq, -1) for o in outs[:3]]
            parts = [_dense_attention(qc, kc, vc).reshape(m, C_QW)]
            w_out = w_out_odd[i]
        x2d, moe_rt = _moe_block(x2d, parts, w_out, ffn_norm[layer], w_router[layer],
                                 w_gate[layer], w_up[layer], w_down[layer], batch, seq)
    return _merge(x2d, moe_rt).reshape(batch, seq, D_MODEL)
```

```python
import functools

import numpy as np
import jax
import jax.numpy as jnp
from jax import lax
from jax.experimental import pallas as pl
from jax.experimental.pallas import tpu as pltpu

D_MODEL = 1024
HEAD_DIM = 64
GRID_W = 64
ROPE_THETA = 10000.0
EPS = 1e-6
NEG_INF = -1e30
LOG2E = 1.4426950408889634

A_HEADS = 8
NA_WIN_ROWS = 8
NA_WIN_COLS = 16
B_HEADS = 8
B_KV_HEADS = 2
B_WINDOW = 128
C_HEADS = 16
C_KV_HEADS = 4
N_EXPERTS = 16
EC_CAPACITY_FACTOR = 2
D_FF = 2048

A_W = A_HEADS * HEAD_DIM
B_QW = B_HEADS * HEAD_DIM
B_KVW = B_KV_HEADS * HEAD_DIM
C_QW = C_HEADS * HEAD_DIM
C_KVW = C_KV_HEADS * HEAD_DIM

SUBLANES = 8
LANES = 128
ROW_TILES = D_MODEL // LANES
VMEM_LIMIT = 56 * 1024 * 1024

TM = 512
NA_QROWS = 4
NA_KROWS = 12
SWA_Q = 256
SWA_K = 512
DENSE_Q = 256
DENSE_UNROLL = 4

F32 = jnp.float32
BF16 = jnp.bfloat16


def _cparams(sem):
    return pltpu.CompilerParams(dimension_semantics=sem, vmem_limit_bytes=VMEM_LIMIT)


def _to_rowtile(src_ref, dst_ref, rows):
    def body(i, carry):
        r0 = pl.multiple_of(i * SUBLANES, SUBLANES)
        base = i * (SUBLANES * ROW_TILES)
        for c in range(ROW_TILES):
            dst_ref[pl.ds(base + c, SUBLANES, stride=ROW_TILES), :] = (
                src_ref[pl.ds(r0, SUBLANES), c * LANES:(c + 1) * LANES])
        return carry
    lax.fori_loop(0, rows // SUBLANES, body, 0)


def _from_rowtile(src_ref, dst_ref, rows):
    def body(i, carry):
        r0 = pl.multiple_of(i * SUBLANES, SUBLANES)
        base = i * (SUBLANES * ROW_TILES)
        for c in range(ROW_TILES):
            dst_ref[pl.ds(r0, SUBLANES), c * LANES:(c + 1) * LANES] = (
                src_ref[pl.ds(base + c, SUBLANES, stride=ROW_TILES), :])
        return carry
    lax.fori_loop(0, rows // SUBLANES, body, 0)


def _rms_rows(x):
    return x * lax.rsqrt(jnp.mean(x * x, axis=-1, keepdims=True) + EPS)


PACK_TILES = ROW_TILES // 2
HALF_D = D_MODEL // 2
U32 = jnp.uint32


def _pack_pair(lo, hi):
    lo_bits = pltpu.bitcast(lo.astype(BF16).astype(F32), U32) >> 16
    hi_bits = pltpu.bitcast(hi.astype(BF16).astype(F32), U32) & jnp.uint32(0xFFFF0000)
    return lo_bits | hi_bits


def _unpack_pair(words):
    lo = pltpu.bitcast(words << 16, F32)
    hi = pltpu.bitcast(words & jnp.uint32(0xFFFF0000), F32)
    return lo, hi


def _to_packed(src_ref, dst_ref, rows):
    def body(i, carry):
        r0 = pl.multiple_of(i * SUBLANES, SUBLANES)
        base = i * (SUBLANES * PACK_TILES)
        for c in range(PACK_TILES):
            lo = src_ref[pl.ds(r0, SUBLANES), c * LANES:(c + 1) * LANES]
            hi = src_ref[pl.ds(r0, SUBLANES), HALF_D + c * LANES:HALF_D + (c + 1) * LANES]
            dst_ref[pl.ds(base + c, SUBLANES, stride=PACK_TILES), :] = _pack_pair(lo, hi)
        return carry
    lax.fori_loop(0, rows // SUBLANES, body, 0)


def _norm_proj_kernel(*refs, sections, has_moe, half):
    it = iter(refs)
    x_ref = next(it)
    moe_ref = next(it) if has_moe else None
    gain_ref = next(it)
    w_ref = next(it)
    cg_ref = next(it)
    cos_ref = next(it)
    sin_ref = next(it)
    bd_ref = next(it)
    n_out = len({s[3] for s in sections})
    outs = [next(it) for _ in range(n_out)]
    xo_ref = next(it) if has_moe else None
    xs_ref = next(it) if has_moe else None

    if has_moe:
        _from_rowtile(moe_ref, xs_ref, TM)
        x = x_ref[...] + xs_ref[...]
        xo_ref[...] = x
    else:
        x = x_ref[...]
    hb = (_rms_rows(x) * gain_ref[...]).astype(BF16)

    ys = [jnp.dot(hb, w_ref[:, c0:c0 + width], preferred_element_type=F32)
          for (c0, width, _, _, _) in sections]
    sss = []
    for y, (c0, width, kind, _, _) in zip(ys, sections):
        if kind == "plain":
            sss.append(None)
            continue
        y2 = y * y
        hi = y2.astype(BF16)
        lo = (y2 - hi.astype(F32)).astype(BF16)
        bd = bd_ref[:width, :width]
        sss.append(jnp.dot(hi, bd, preferred_element_type=F32)
                   + jnp.dot(lo, bd, preferred_element_type=F32))
    for y, ss, (c0, width, kind, oi, oc) in zip(ys, sss, sections):
        if kind != "plain":
            y = y * lax.rsqrt(ss + EPS) * cg_ref[:, c0:c0 + width]
        if kind == "rope":
            lane = lax.broadcasted_iota(jnp.int32, (1, width), 1)
            first = (lane % (2 * half)) < half
            part = jnp.where(first, pltpu.roll(y, width - half, 1), pltpu.roll(y, half, 1))
            y = y * cos_ref[:, :width] + part * sin_ref[:, :width]
        outs[oi][:, oc:oc + width] = y.astype(BF16)


def _norm_proj(x2d, moe_rt, gain, w_bf16, colgain, cos_t, sin_t, sections, out_widths, half, seq):
    m = x2d.shape[0]
    n_in = w_bf16.shape[1]
    has_moe = moe_rt is not None
    bd = jnp.asarray(np.kron(np.eye(4), np.full((HEAD_DIM, HEAD_DIM), 1.0 / HEAD_DIM)), BF16)
    tiles_per_seq = seq // TM
    in_specs = [pl.BlockSpec((TM, D_MODEL), lambda i: (i, 0))]
    args = [x2d]
    if has_moe:
        in_specs.append(pl.BlockSpec((TM * ROW_TILES, LANES), lambda i: (i, 0)))
        args.append(moe_rt)
    in_specs += [
        pl.BlockSpec((1, D_MODEL), lambda i: (0, 0)),
        pl.BlockSpec((D_MODEL, n_in), lambda i: (0, 0)),
        pl.BlockSpec((1, n_in), lambda i: (0, 0)),
        pl.BlockSpec((TM, 256), lambda i: (i % tiles_per_seq, 0)),
        pl.BlockSpec((TM, 256), lambda i: (i % tiles_per_seq, 0)),
        pl.BlockSpec((256, 256), lambda i: (0, 0)),
    ]
    args += [gain.reshape(1, D_MODEL), w_bf16, colgain, cos_t, sin_t, bd]
    out_shape = [jax.ShapeDtypeStruct((m, wd), BF16) for wd in out_widths]
    out_specs = [pl.BlockSpec((TM, wd), lambda i: (i, 0)) for wd in out_widths]
    scratch = []
    if has_moe:
        out_shape.append(jax.ShapeDtypeStruct((m, D_MODEL), F32))
        out_specs.append(pl.BlockSpec((TM, D_MODEL), lambda i: (i, 0)))
        scratch.append(pltpu.VMEM((TM, D_MODEL), F32))
    kern = functools.partial(_norm_proj_kernel, sections=tuple(sections), has_moe=has_moe, half=half)
    return pl.pallas_call(
        kern, grid=(m // TM,), in_specs=in_specs, out_specs=out_specs, out_shape=out_shape,
        scratch_shapes=scratch, compiler_params=_cparams(("parallel",)),
        name="norm_proj_moe" if has_moe else "norm_proj")(*args)


def _na_kernel(q_ref, k_ref, v_ref, bias_ref, o_ref):
    nq = NA_QROWS * GRID_W
    nk = NA_KROWS * GRID_W
    rows = q_ref.shape[1] // GRID_W
    n_blocks = rows // NA_QROWS
    lane = lax.broadcasted_iota(jnp.int32, (1, LANES), 1)
    head0 = lane < HEAD_DIM

    for qb in range(n_blocks):
        r0 = qb * NA_QROWS
        start = min(max(r0 - NA_WIN_ROWS // 2, 0), rows - NA_KROWS)
        pat = 0 if qb == 0 else (2 if qb == n_blocks - 1 else 1)
        qs = r0 * GRID_W
        ks = start * GRID_W
        q = q_ref[0, qs:qs + nq, :]
        k = k_ref[0, ks:ks + nk, :]
        v = v_ref[0, ks:ks + nk, :]
        acc = None
        for hh in range(2):
            sel = head0 if hh == 0 else jnp.logical_not(head0)
            qm = jnp.where(sel, q, jnp.zeros_like(q))
            s = lax.dot_general(qm, k, (((1,), (1,)), ((), ())), preferred_element_type=F32)
            s = s + bias_ref[pat, hh]
            m = jnp.max(s, axis=-1, keepdims=True)
            p = jnp.exp2(s - m)
            l = jnp.sum(p, axis=-1, keepdims=True)
            o = jnp.dot(p.astype(BF16), v, preferred_element_type=F32) / l
            acc = o if acc is None else jnp.where(sel, o, acc)
        o_ref[0, qs:qs + nq, :] = acc.astype(BF16)


def _na_bias_tables(rel_bias, rows):
    n_dr, n_dc = 2 * NA_WIN_ROWS - 1, 2 * NA_WIN_COLS - 1
    sel_r = np.zeros((3, NA_QROWS, NA_KROWS, n_dr), np.float32)
    in_r = np.zeros((3, NA_QROWS, NA_KROWS), np.float32)
    for p, r0 in enumerate((0, 2 * NA_QROWS, rows - NA_QROWS)):
        start = int(np.clip(r0 - NA_WIN_ROWS // 2, 0, rows - NA_KROWS))
        for lr in range(NA_QROWS):
            r = r0 + lr
            rs = int(np.clip(r - NA_WIN_ROWS // 2, 0, rows - NA_WIN_ROWS))
            for kl in range(NA_KROWS):
                kr = start + kl
                if rs <= kr < rs + NA_WIN_ROWS:
                    in_r[p, lr, kl] = 1.0
                    sel_r[p, lr, kl, kr - r + NA_WIN_ROWS - 1] = 1.0
    sel_c = np.zeros((GRID_W, GRID_W, n_dc), np.float32)
    in_c = np.zeros((GRID_W, GRID_W), np.float32)
    for c in range(GRID_W):
        cs = int(np.clip(c - NA_WIN_COLS // 2, 0, GRID_W - NA_WIN_COLS))
        for kc in range(cs, cs + NA_WIN_COLS):
            in_c[c, kc] = 1.0
            dc = int(np.clip(kc - c, -(NA_WIN_COLS - 1), NA_WIN_COLS - 1))
            sel_c[c, kc, dc + NA_WIN_COLS - 1] = 1.0
    hp = lax.Precision.HIGHEST
    rows_sel = jnp.einsum('plkd,hde->phlke', jnp.asarray(sel_r), rel_bias.astype(F32), precision=hp)
    vals = jnp.einsum('phlke,cje->phlckj', rows_sel, jnp.asarray(sel_c), precision=hp)
    inside = jnp.asarray(in_r[:, None, :, None, :, None] * in_c[None, None, None, :, None, :])
    tab = jnp.where(inside > 0.5, vals * LOG2E, NEG_INF)
    return tab.reshape(3, rel_bias.shape[0], NA_QROWS * GRID_W, NA_KROWS * GRID_W)


def _na_attention(q, k, v, rel_bias):
    b, s, _ = q.shape
    rows = s // GRID_W
    assert rows % NA_QROWS == 0 and rows >= NA_KROWS and rows // NA_QROWS >= 3
    nq, nk = NA_QROWS * GRID_W, NA_KROWS * GRID_W
    bias = _na_bias_tables(rel_bias, rows)
    blk = pl.BlockSpec((1, s, LANES), lambda bi, hp: (bi, 0, hp))
    return pl.pallas_call(
        _na_kernel, grid=(b, A_HEADS // 2),
        in_specs=[blk, blk, blk, pl.BlockSpec((3, 2, nq, nk), lambda bi, hp: (0, hp, 0, 0))],
        out_specs=blk, out_shape=jax.ShapeDtypeStruct((b, s, A_W), BF16),
        compiler_params=_cparams(("parallel", "arbitrary")), name="na_attention")(q, k, v, bias)


def _swa_kernel(sink_ref, q_ref, k_ref, v_ref, rep_ref, o_ref, kt_ref, vt_ref):
    s_len = q_ref.shape[1]
    n_groups = B_HEADS // B_KV_HEADS
    gw = n_groups * HEAD_DIM
    lane = lax.broadcasted_iota(jnp.int32, (1, gw), 1)
    rel = (lax.broadcasted_iota(jnp.int32, (SWA_Q, SWA_K), 0)
           - lax.broadcasted_iota(jnp.int32, (SWA_Q, SWA_K), 1))

    for g in range(B_KV_HEADS):
        kt_ref[...] = jnp.dot(k_ref[0], rep_ref[g], preferred_element_type=F32).astype(BF16)
        vt_ref[...] = jnp.dot(v_ref[0], rep_ref[g], preferred_element_type=F32).astype(BF16)

        for i in range(s_len // SWA_Q):
            q0 = i * SWA_Q
            k0 = min(max(q0 - B_WINDOW, 0), s_len - SWA_K)
            q = q_ref[0, q0:q0 + SWA_Q, g * gw:(g + 1) * gw]
            kt = kt_ref[k0:k0 + SWA_K, :]
            vt = vt_ref[k0:k0 + SWA_K, :]
            valid = jnp.abs(rel + (q0 - k0)) <= B_WINDOW
            acc = jnp.zeros((SWA_Q, gw), F32)
            for j in range(n_groups):
                sel = (lane >= j * HEAD_DIM) & (lane < (j + 1) * HEAD_DIM)
                sink = sink_ref[g * n_groups + j]
                qm = jnp.where(sel, q, jnp.zeros_like(q))
                s = lax.dot_general(qm, kt, (((1,), (1,)), ((), ())), preferred_element_type=F32)
                s = jnp.where(valid, s, NEG_INF)
                m = jnp.maximum(jnp.max(s, axis=-1, keepdims=True), sink)
                e = jnp.exp2(s - m)
                den = jnp.sum(e, axis=-1, keepdims=True) + jnp.exp2(sink - m)
                o = jnp.dot(e.astype(BF16), vt, preferred_element_type=F32) / den
                acc = jnp.where(sel, o, acc)
            o_ref[0, q0:q0 + SWA_Q, g * gw:(g + 1) * gw] = acc.astype(BF16)


def _replication_matrices(n_kv):
    rep = np.zeros((n_kv, n_kv * HEAD_DIM, 4 * HEAD_DIM), np.float32)
    for g in range(n_kv):
        for j in range(4 * HEAD_DIM):
            rep[g, g * HEAD_DIM + j % HEAD_DIM, j] = 1.0
    return jnp.asarray(rep, BF16)


def _swa_attention(q, k, v, sink):
    b, s, _ = q.shape
    assert s % SWA_Q == 0 and s >= SWA_K and SWA_K >= SWA_Q + 2 * B_WINDOW
    rep = _replication_matrices(B_KV_HEADS)
    return pl.pallas_call(
        _swa_kernel, grid=(b,),
        in_specs=[pl.BlockSpec(memory_space=pltpu.SMEM),
                  pl.BlockSpec((1, s, B_QW), lambda bi: (bi, 0, 0)),
                  pl.BlockSpec((1, s, B_KVW), lambda bi: (bi, 0, 0)),
                  pl.BlockSpec((1, s, B_KVW), lambda bi: (bi, 0, 0)),
                  pl.BlockSpec((B_KV_HEADS, B_KVW, 4 * HEAD_DIM), lambda bi: (0, 0, 0))],
        out_specs=pl.BlockSpec((1, s, B_QW), lambda bi: (bi, 0, 0)),
        out_shape=jax.ShapeDtypeStruct((b, s, B_QW), BF16),
        scratch_shapes=[pltpu.VMEM((s, 4 * HEAD_DIM), BF16), pltpu.VMEM((s, 4 * HEAD_DIM), BF16)],
        compiler_params=_cparams(("parallel",)), name="swa_attention")(sink.astype(F32) * LOG2E, q, k, v, rep)


def _dense_kernel(q_ref, k_ref, v_ref, rep_ref, o_ref, kt_ref, vt_ref):
    s_len = q_ref.shape[1]
    gw = 4 * HEAD_DIM
    lane = lax.broadcasted_iota(jnp.int32, (1, gw), 1)
    kt_ref[...] = jnp.dot(k_ref[0], rep_ref[0], preferred_element_type=F32).astype(BF16)
    vt_ref[...] = jnp.dot(v_ref[0], rep_ref[0], preferred_element_type=F32).astype(BF16)

    def body(i, carry):
        for u in range(DENSE_UNROLL):
            q0 = pl.multiple_of((i * DENSE_UNROLL + u) * DENSE_Q, DENSE_Q)
            q = q_ref[0, pl.ds(q0, DENSE_Q), :]
            acc = jnp.zeros((DENSE_Q, gw), F32)
            for j in range(4):
                sel = (lane >= j * HEAD_DIM) & (lane < (j + 1) * HEAD_DIM)
                qm = jnp.where(sel, q, jnp.zeros_like(q))
                s = lax.dot_general(qm, kt_ref[...], (((1,), (1,)), ((), ())), preferred_element_type=F32)
                m = jnp.max(s, axis=-1, keepdims=True)
                p = jnp.exp2(s - m)
                l = jnp.sum(p, axis=-1, keepdims=True)
                o = jnp.dot(p.astype(BF16), vt_ref[...], preferred_element_type=F32) / l
                acc = jnp.where(sel, o, acc)
            o_ref[0, pl.ds(q0, DENSE_Q), :] = acc.astype(BF16)
        return carry

    lax.fori_loop(0, s_len // (DENSE_Q * DENSE_UNROLL), body, 0)


def _dense_attention(q, k, v):
    b, s, _ = q.shape
    gw = 4 * HEAD_DIM
    rep = _replication_matrices(C_KV_HEADS)
    return pl.pallas_call(
        _dense_kernel, grid=(b, C_KV_HEADS),
        in_specs=[pl.BlockSpec((1, s, gw), lambda bi, g: (bi, 0, g)),
                  pl.BlockSpec((1, s, C_KVW), lambda bi, g: (bi, 0, 0)),
                  pl.BlockSpec((1, s, C_KVW), lambda bi, g: (bi, 0, 0)),
                  pl.BlockSpec((1, C_KVW, gw), lambda bi, g: (g, 0, 0))],
        out_specs=pl.BlockSpec((1, s, gw), lambda bi, g: (bi, 0, g)),
        out_shape=jax.ShapeDtypeStruct((b, s, C_QW), BF16),
        scratch_shapes=[pltpu.VMEM((s, gw), BF16), pltpu.VMEM((s, gw), BF16)],
        compiler_params=_cparams(("parallel", "arbitrary")), name="dense_attention")(q, k, v, rep)


def _out_proj_kernel(*refs, n_parts):
    it = iter(refs)
    x_ref = next(it)
    parts = [next(it) for _ in range(n_parts)]
    w_ref = next(it)
    gain_ref = next(it)
    wr_ref = next(it)
    xo_ref = next(it)
    hpk_ref = next(it)
    aff_ref = next(it)
    afft_ref = next(it)
    h_ref = next(it)

    y = x_ref[...]
    c0 = 0
    for p_ref in parts:
        wp = p_ref.shape[1]
        y = y + jnp.dot(p_ref[...], w_ref[c0:c0 + wp, :], preferred_element_type=F32)
        c0 += wp
    xo_ref[...] = y
    h = _rms_rows(y) * gain_ref[...]
    h_ref[...] = h
    _to_packed(h_ref, hpk_ref, TM)

    h_hi = h.astype(BF16)
    h_lo = (h - h_hi.astype(F32)).astype(BF16)
    r_hi = jnp.dot(h_hi, wr_ref[...], preferred_element_type=F32)
    r_lo = jnp.dot(h_lo, wr_ref[:, :LANES], preferred_element_type=F32)
    lane = lax.broadcasted_iota(jnp.int32, (1, LANES), 1)
    logits = r_hi[:, :LANES] + r_hi[:, LANES:] + r_lo
    logits = jnp.where(lane < N_EXPERTS, logits, NEG_INF)
    e = jnp.exp(logits - jnp.max(logits, axis=-1, keepdims=True))
    aff = e / jnp.sum(e, axis=-1, keepdims=True)
    aff_ref[...] = aff
    afft_ref[0] = jnp.transpose(aff)[:N_EXPERTS, :]


def _out_proj(x2d, parts, w_out_bf16, ffn_gain, w_router, batch, seq):
    m = x2d.shape[0]
    tiles_per_seq = seq // TM
    wr = w_router.astype(F32)
    wr_hi = wr.astype(BF16)
    wr_lo = (wr - wr_hi.astype(F32)).astype(BF16)
    wr_split = jnp.zeros((D_MODEL, 2 * LANES), BF16)
    wr_split = wr_split.at[:, :N_EXPERTS].set(wr_hi).at[:, LANES:LANES + N_EXPERTS].set(wr_lo)
    in_specs = [pl.BlockSpec((TM, D_MODEL), lambda i: (i, 0))]
    in_specs += [pl.BlockSpec((TM, p.shape[1]), lambda i: (i, 0)) for p in parts]
    in_specs += [pl.BlockSpec((D_MODEL, D_MODEL), lambda i: (0, 0)),
                 pl.BlockSpec((1, D_MODEL), lambda i: (0, 0)),
                 pl.BlockSpec((D_MODEL, 2 * LANES), lambda i: (0, 0))]
    out_shape = [jax.ShapeDtypeStruct((m, D_MODEL), F32),
                 jax.ShapeDtypeStruct((m * PACK_TILES, LANES), U32),
                 jax.ShapeDtypeStruct((m, LANES), F32),
                 jax.ShapeDtypeStruct((batch, N_EXPERTS, seq), F32)]
    out_specs = [pl.BlockSpec((TM, D_MODEL), lambda i: (i, 0)),
                 pl.BlockSpec((TM * PACK_TILES, LANES), lambda i: (i, 0)),
                 pl.BlockSpec((TM, LANES), lambda i: (i, 0)),
                 pl.BlockSpec((1, N_EXPERTS, TM), lambda i: (i // tiles_per_seq, 0, i % tiles_per_seq))]
    kern = functools.partial(_out_proj_kernel, n_parts=len(parts))
    return pl.pallas_call(
        kern, grid=(m // TM,), in_specs=in_specs, out_specs=out_specs, out_shape=out_shape,
        scratch_shapes=[pltpu.VMEM((TM, D_MODEL), F32)],
        compiler_params=_cparams(("parallel",)), name="out_proj_router")(
            x2d, *parts, w_out_bf16, ffn_gain.reshape(1, D_MODEL), wr_split)


def _topk_kernel(afft_ref, aff_ref, ids_ref, gate_ref, res_ref, *, cap):
    s_len = afft_ref.shape[2]
    at = afft_ref[0]
    bits = pltpu.bitcast(at, jnp.int32)

    def count(mask):
        return jnp.sum(jnp.where(mask, 1.0, 0.0), axis=1, keepdims=True)

    def bit_step(k, thr):
        cand = thr | jnp.left_shift(jnp.int32(1), 30 - k)
        return jnp.where(count(bits >= cand) >= cap, cand, thr)

    thr = lax.fori_loop(0, 31, bit_step, jnp.zeros((N_EXPERTS, 1), jnp.int32))
    gt = bits > thr
    eq = bits == thr
    need = cap - count(gt)

    ri = lax.broadcasted_iota(jnp.int32, (LANES, LANES), 0)
    ci = lax.broadcasted_iota(jnp.int32, (LANES, LANES), 1)
    upper = jnp.where(ri < ci, 1.0, 0.0).astype(BF16)

    def prefix(mask_f32):
        out = []
        run = jnp.zeros((N_EXPERTS, 1), F32)
        for c in range(s_len // LANES):
            blk = mask_f32[:, c * LANES:(c + 1) * LANES]
            out.append(jnp.dot(blk.astype(BF16), upper, preferred_element_type=F32) + run)
            run = run + jnp.sum(blk, axis=1, keepdims=True)
        return out

    eq_f = jnp.where(eq, 1.0, 0.0)
    eq_rank = prefix(eq_f)
    sel_tiles = []
    for c in range(s_len // LANES):
        sl = slice(c * LANES, (c + 1) * LANES)
        sel_tiles.append(jnp.where(gt[:, sl] | (eq[:, sl] & (eq_rank[c] < need)), 1.0, 0.0))
    sel_f = jnp.concatenate(sel_tiles, axis=1)
    pos = jnp.concatenate(prefix(sel_f), axis=1)
    key = jnp.where(sel_f > 0.5, pos, -1.0)

    a = aff_ref[0]
    a_hi = a.astype(BF16).astype(F32)
    r1 = a - a_hi
    a_mid = r1.astype(BF16).astype(F32)
    a_lo = r1 - a_mid
    lane = lax.broadcasted_iota(jnp.int32, (s_len, LANES), 1)
    tok = lax.broadcasted_iota(jnp.int32, (s_len, LANES), 0)
    packed = a_hi + pltpu.roll(a_mid, N_EXPERTS, 1) + pltpu.roll(a_lo, 2 * N_EXPERTS, 1)
    packed = jnp.where(lane == 64, (tok // 64).astype(F32), packed)
    packed = jnp.where(lane == 65, (tok % 64).astype(F32), packed)
    rmat = packed.astype(BF16)

    slot = lax.broadcasted_iota(jnp.int32, (cap, 1), 0).astype(F32)
    for e in range(N_EXPERTS):
        onehot_t = jnp.where(key[e:e + 1, :] == slot, 1.0, 0.0).astype(BF16)
        res_ref[...] = jnp.dot(onehot_t, rmat, preferred_element_type=F32)
        res = jnp.transpose(res_ref[...])
        ids_ref[0, e:e + 1, :] = (res[64:65, :] * 64.0 + res[65:66, :]).astype(jnp.int32)
        gate_ref[0, e:e + 1, :] = (res[e:e + 1, :] + res[N_EXPERTS + e:N_EXPERTS + e + 1, :]
                                   + res[2 * N_EXPERTS + e:2 * N_EXPERTS + e + 1, :])


def _topk(aff_t, aff, cap):
    b, _, s = aff_t.shape
    kern = functools.partial(_topk_kernel, cap=cap)
    return pl.pallas_call(
        kern, grid=(b,),
        in_specs=[pl.BlockSpec((1, N_EXPERTS, s), lambda bi: (bi, 0, 0)),
                  pl.BlockSpec((1, s, LANES), lambda bi: (bi, 0, 0))],
        out_specs=[pl.BlockSpec((1, N_EXPERTS, cap), lambda bi: (bi, 0, 0)),
                   pl.BlockSpec((1, N_EXPERTS, cap), lambda bi: (bi, 0, 0))],
        out_shape=[jax.ShapeDtypeStruct((b, N_EXPERTS, cap), jnp.int32),
                   jax.ShapeDtypeStruct((b, N_EXPERTS, cap), F32)],
        scratch_shapes=[pltpu.VMEM((cap, LANES), F32)],
        compiler_params=_cparams(("parallel",)), name="expert_topk")(aff_t, aff)


MOE_GROUP = 8


MOE_EXPERTS_PER_STEP = 4
FFN_ROWS = 512
FFN_F_CHUNK = 512


def _gather_kernel(ids_ref, h_ref, o_ref, *, cap):
    for el in range(MOE_EXPERTS_PER_STEP):
        def body(jg, carry, el=el):
            for u in range(MOE_GROUP):
                j = jg * MOE_GROUP + u
                o_ref[el, j] = h_ref[0, ids_ref[0, 0, el * cap + j]]
            return carry
        lax.fori_loop(0, cap // MOE_GROUP, body, 0)


def _ffn_kernel(xe_ref, wg_ref, wu_ref, wd_ref, o_ref, xb_ref):
    pair = 2 * SUBLANES
    for i in range(FFN_ROWS // pair):
        for c in range(PACK_TILES):
            base = i * pair * PACK_TILES + c
            lo_a, hi_a = _unpack_pair(xe_ref[0, pl.ds(base, SUBLANES, stride=PACK_TILES), :])
            lo_b, hi_b = _unpack_pair(
                xe_ref[0, pl.ds(base + SUBLANES * PACK_TILES, SUBLANES, stride=PACK_TILES), :])
            rows = slice(i * pair, (i + 1) * pair)
            xb_ref[rows, c * LANES:(c + 1) * LANES] = jnp.concatenate([lo_a, lo_b], axis=0).astype(BF16)
            xb_ref[rows, HALF_D + c * LANES:HALF_D + (c + 1) * LANES] = (
                jnp.concatenate([hi_a, hi_b], axis=0).astype(BF16))
    xe = xb_ref[...]
    y = None
    for f0 in range(0, D_FF, FFN_F_CHUNK):
        hg = jnp.dot(xe, wg_ref[0, 0, :, f0:f0 + FFN_F_CHUNK], preferred_element_type=F32)
        hu = jnp.dot(xe, wu_ref[0, 0, :, f0:f0 + FFN_F_CHUNK], preferred_element_type=F32)
        hid = (hg * (1.0 / (1.0 + jnp.exp(-hg))) * hu).astype(BF16)
        part = jnp.dot(hid, wd_ref[0, 0, f0:f0 + FFN_F_CHUNK, :], preferred_element_type=F32)
        y = part if y is None else y + part
    for i in range(FFN_ROWS // SUBLANES):
        for c in range(ROW_TILES):
            o_ref[0, pl.ds(i * SUBLANES * ROW_TILES + c, SUBLANES, stride=ROW_TILES), :] = (
                y[i * SUBLANES:(i + 1) * SUBLANES, c * LANES:(c + 1) * LANES])


def _scatter_kernel(ids_ref, gate_ref, y_ref, o_ref, *, cap):
    @pl.when(pl.program_id(1) == 0)
    def _():
        o_ref[...] = jnp.zeros_like(o_ref)

    for el in range(MOE_EXPERTS_PER_STEP):
        def body(jg, carry, el=el):
            rows = []
            for u in range(MOE_GROUP):
                j = jg * MOE_GROUP + u
                rows.append(pl.multiple_of(ids_ref[0, 0, el * cap + j] * ROW_TILES, ROW_TILES))
            vals = []
            for u in range(MOE_GROUP):
                j = jg * MOE_GROUP + u
                src = pl.multiple_of(j * ROW_TILES, ROW_TILES)
                gate = gate_ref[0, 0, el * cap + j]
                vals.append(o_ref[0, pl.ds(rows[u], ROW_TILES), :]
                            + y_ref[el, pl.ds(src, ROW_TILES), :] * gate)
            for u in range(MOE_GROUP):
                o_ref[0, pl.ds(rows[u], ROW_TILES), :] = vals[u]
            return carry
        lax.fori_loop(0, cap // MOE_GROUP, body, 0)


def _moe(ids, gates, h_pk, wg, wu, wd, layer, cap):
    b = ids.shape[0]
    seq = h_pk.shape[0] // (b * PACK_TILES)
    n_eg = N_EXPERTS // MOE_EXPERTS_PER_STEP
    step_ids = MOE_EXPERTS_PER_STEP * cap
    ids_g = ids.reshape(b * n_eg, 1, step_ids)
    gates_g = gates.reshape(b * n_eg, 1, step_ids)
    smem_spec = pl.BlockSpec((1, 1, step_ids), lambda bi, eg: (bi * n_eg + eg, 0, 0), memory_space=pltpu.SMEM)

    xe_pk = pl.pallas_call(
        functools.partial(_gather_kernel, cap=cap), grid=(b, n_eg),
        in_specs=[smem_spec, pl.BlockSpec((1, seq, PACK_TILES, LANES), lambda bi, eg: (bi, 0, 0, 0))],
        out_specs=pl.BlockSpec((MOE_EXPERTS_PER_STEP, cap, PACK_TILES, LANES), lambda bi, eg: (eg, bi, 0, 0)),
        out_shape=jax.ShapeDtypeStruct((N_EXPERTS, b * cap, PACK_TILES, LANES), U32),
        compiler_params=_cparams(("parallel", "arbitrary")), name="moe_gather")(
            ids_g, h_pk.reshape(b, seq, PACK_TILES, LANES))

    assert (b * cap) % FFN_ROWS == 0
    slot_rows = cap * ROW_TILES
    ye_rt = pl.pallas_call(
        _ffn_kernel, grid=(N_EXPERTS, b * cap // FFN_ROWS),
        in_specs=[pl.BlockSpec((1, FFN_ROWS * PACK_TILES, LANES), lambda e, i: (e, i, 0)),
                  pl.BlockSpec((1, 1, D_MODEL, D_FF), lambda e, i: (layer, e, 0, 0)),
                  pl.BlockSpec((1, 1, D_MODEL, D_FF), lambda e, i: (layer, e, 0, 0)),
                  pl.BlockSpec((1, 1, D_FF, D_MODEL), lambda e, i: (layer, e, 0, 0))],
        out_specs=pl.BlockSpec((1, FFN_ROWS * ROW_TILES, LANES), lambda e, i: (e, i, 0)),
        out_shape=jax.ShapeDtypeStruct((N_EXPERTS, b * slot_rows, LANES), F32),
        scratch_shapes=[pltpu.VMEM((FFN_ROWS, D_MODEL), BF16)],
        compiler_params=_cparams(("parallel", "arbitrary")), name="moe_ffn")(
            xe_pk.reshape(N_EXPERTS, b * cap * PACK_TILES, LANES), wg, wu, wd)

    rt_rows = seq * ROW_TILES
    return pl.pallas_call(
        functools.partial(_scatter_kernel, cap=cap), grid=(b, n_eg),
        in_specs=[smem_spec, smem_spec,
                  pl.BlockSpec((MOE_EXPERTS_PER_STEP, slot_rows, LANES), lambda bi, eg: (eg, bi, 0))],
        out_specs=pl.BlockSpec((1, rt_rows, LANES), lambda bi, eg: (bi, 0, 0)),
        out_shape=jax.ShapeDtypeStruct((b, rt_rows, LANES), F32),
        compiler_params=_cparams(("parallel", "arbitrary")), name="moe_scatter")(ids_g, gates_g, ye_rt)


def _merge_kernel(x_ref, moe_ref, o_ref, xs_ref):
    _from_rowtile(moe_ref, xs_ref, TM)
    o_ref[...] = x_ref[...] + xs_ref[...]


def _merge(x2d, moe_rt):
    m = x2d.shape[0]
    return pl.pallas_call(
        _merge_kernel, grid=(m // TM,),
        in_specs=[pl.BlockSpec((TM, D_MODEL), lambda i: (i, 0)),
                  pl.BlockSpec((TM * ROW_TILES, LANES), lambda i: (i, 0))],
        out_specs=pl.BlockSpec((TM, D_MODEL), lambda i: (i, 0)),
        out_shape=jax.ShapeDtypeStruct((m, D_MODEL), F32),
        scratch_shapes=[pltpu.VMEM((TM, D_MODEL), F32)],
        compiler_params=_cparams(("parallel",)), name="residual_merge")(x2d, moe_rt)


def _rope_tables(seq):
    t = jnp.arange(seq).astype(F32)
    inv = jnp.power(jnp.float32(ROPE_THETA), -jnp.arange(0, HEAD_DIM, 2, dtype=F32) / HEAD_DIM)
    ang = t[:, None] * inv[None, :]
    cos = jnp.concatenate([jnp.cos(ang), jnp.cos(ang)], axis=1)
    sin = jnp.concatenate([-jnp.sin(ang), jnp.sin(ang)], axis=1)
    return jnp.tile(cos, (1, 4)), jnp.tile(sin, (1, 4))


def _axial_tables(seq):
    t = jnp.arange(seq)
    half = HEAD_DIM // 2
    inv = jnp.power(jnp.float32(ROPE_THETA), -jnp.arange(0, half, 2, dtype=F32) / half)
    ang_r = (t // GRID_W).astype(F32)[:, None] * inv[None, :]
    ang_c = (t % GRID_W).astype(F32)[:, None] * inv[None, :]
    cos = jnp.concatenate([jnp.cos(ang_r), jnp.cos(ang_r), jnp.cos(ang_c), jnp.cos(ang_c)], axis=1)
    sin = jnp.concatenate([-jnp.sin(ang_r), jnp.sin(ang_r), -jnp.sin(ang_c), jnp.sin(ang_c)], axis=1)
    return jnp.tile(cos, (1, 4)), jnp.tile(sin, (1, 4))


def _col_gain(pieces):
    cols = []
    for gain, n_heads, scale in pieces:
        if gain is None:
            cols.append(jnp.ones((n_heads * HEAD_DIM,), F32))
        else:
            cols.append(jnp.tile(gain.astype(F32) * scale, n_heads))
    return jnp.concatenate(cols)[None, :]


def _chunks(start, width, kind, oi):
    out = []
    step = 256 if width % 256 == 0 else LANES
    for c in range(0, width, step):
        out.append((start + c, step, kind, oi, c))
    return out


def _moe_block(x2d, parts, w_out, ffn_gain, w_router, wg, wu, wd, layer, batch, seq):
    cap = EC_CAPACITY_FACTOR * seq // N_EXPERTS
    x_new, h_pk, aff, aff_t = _out_proj(x2d, parts, w_out.astype(BF16), ffn_gain, w_router, batch, seq)
    ids, gates = _topk(aff_t, aff.reshape(batch, seq, LANES), cap)
    moe_rt = _moe(ids, gates, h_pk, wg, wu, wd, layer, cap)
    return x_new, moe_rt.reshape(batch * seq * ROW_TILES, LANES)


def kernel(x, attn_norm_even, w_in_even, q_norm_a, k_norm_a, rel_bias_a, q_norm_b, k_norm_b, sink_b,
           w_out_even, attn_norm_odd, w_in_odd, q_norm_c, k_norm_c, w_out_odd,
           ffn_norm, w_router, w_gate, w_up, w_down):
    batch, seq, _ = x.shape
    m = batch * seq
    scale = HEAD_DIM ** -0.5 * LOG2E
    x2d = x.reshape(m, D_MODEL)
    moe_rt = None
    depth = ffn_norm.shape[0]
    wg_all, wu_all, wd_all = w_gate.astype(BF16), w_up.astype(BF16), w_down.astype(BF16)
    for layer in range(depth):
        i = layer // 2
        if layer % 2 == 0:
            cos_t, sin_t = _rope_tables(seq)
            colgain = _col_gain([(q_norm_a[i], A_HEADS, scale), (k_norm_a[i], A_HEADS, 1.0),
                                 (None, A_HEADS, 1.0), (q_norm_b[i], B_HEADS, scale),
                                 (k_norm_b[i], B_KV_HEADS, 1.0), (None, B_KV_HEADS, 1.0)])
            sections = (_chunks(0, A_W, "norm", 0) + _chunks(A_W, A_W, "norm", 1)
                        + _chunks(2 * A_W, A_W, "plain", 2) + _chunks(3 * A_W, B_QW, "rope", 3)
                        + _chunks(3 * A_W + B_QW, B_KVW, "rope", 4)
                        + _chunks(3 * A_W + B_QW + B_KVW, B_KVW, "plain", 5))
            widths = [A_W, A_W, A_W, B_QW, B_KVW, B_KVW]
            outs = _norm_proj(x2d, moe_rt, attn_norm_even[i], w_in_even[i].astype(BF16), colgain,
                              cos_t, sin_t, sections, widths, HEAD_DIM // 2, seq)
            if moe_rt is not None:
                x2d = outs[-1]
            qa, ka, va, qb, kb, vb = [o.reshape(batch, seq, -1) for o in outs[:6]]
            out_a = _na_attention(qa, ka, va, rel_bias_a[i])
            out_b = _swa_attention(qb, kb, vb, sink_b[i])
            parts = [out_a.reshape(m, A_W), out_b.reshape(m, B_QW)]
            w_out = w_out_even[i]
        else:
            cos_t, sin_t = _axial_tables(seq)
            colgain = _col_gain([(q_norm_c[i], C_HEADS, scale), (k_norm_c[i], C_KV_HEADS, 1.0),
                                 (None, C_KV_HEADS, 1.0)])
            sections = (_chunks(0, C_QW, "rope", 0) + _chunks(C_QW, C_KVW, "rope", 1)
                        + _chunks(C_QW + C_KVW, C_KVW, "plain", 2))
            widths = [C_QW, C_KVW, C_KVW]
            outs = _norm_proj(x2d, moe_rt, attn_norm_odd[i], w_in_odd[i].astype(BF16), colgain,
                              cos_t, sin_t, sections, widths, HEAD_DIM // 4, seq)
            if moe_rt is not None:
                x2d = outs[-1]
            qc, kc, vc = [o.reshape(batch, seq, -1) for o in outs[:3]]
            parts = [_dense_attention(qc, kc, vc).reshape(m, C_QW)]
            w_out = w_out_odd[i]
        x2d, moe_rt = _moe_block(x2d, parts, w_out, ffn_norm[layer], w_router[layer],
                                 wg_all, wu_all, wd_all, layer, batch, seq)
    return _merge(x2d, moe_rt).reshape(batch, seq, D_MODEL)
```

```python
import functools

import numpy as np
import jax
import jax.numpy as jnp
from jax import lax
from jax.experimental import pallas as pl
from jax.experimental.pallas import tpu as pltpu

D_MODEL = 1024
HEAD_DIM = 64
GRID_W = 64
ROPE_THETA = 10000.0
EPS = 1e-6
NEG_INF = -1e30
LOG2E = 1.4426950408889634

A_HEADS = 8
NA_WIN_ROWS = 8
NA_WIN_COLS = 16
B_HEADS = 8
B_KV_HEADS = 2
B_WINDOW = 128
C_HEADS = 16
C_KV_HEADS = 4
N_EXPERTS = 16
EC_CAPACITY_FACTOR = 2
D_FF = 2048

A_W = A_HEADS * HEAD_DIM
B_QW = B_HEADS * HEAD_DIM
B_KVW = B_KV_HEADS * HEAD_DIM
C_QW = C_HEADS * HEAD_DIM
C_KVW = C_KV_HEADS * HEAD_DIM

SUBLANES = 8
LANES = 128
ROW_TILES = D_MODEL // LANES
VMEM_LIMIT = 56 * 1024 * 1024

TM = 512
NA_QROWS = 4
NA_KROWS = 12
SWA_Q = 256
SWA_K = 512
DENSE_Q = 256
DENSE_UNROLL = 8

F32 = jnp.float32
BF16 = jnp.bfloat16


def _cparams(sem):
    return pltpu.CompilerParams(dimension_semantics=sem, vmem_limit_bytes=VMEM_LIMIT)


def _rms_rows(x):
    return x * lax.rsqrt(jnp.mean(x * x, axis=-1, keepdims=True) + EPS)


PACK_TILES = ROW_TILES // 2
HALF_D = D_MODEL // 2
U32 = jnp.uint32


def _pack_pair(lo, hi):
    lo_bits = pltpu.bitcast(lo.astype(BF16).astype(F32), U32) >> 16
    hi_bits = pltpu.bitcast(hi.astype(BF16).astype(F32), U32) & jnp.uint32(0xFFFF0000)
    return lo_bits | hi_bits


def _unpack_pair(words):
    lo = pltpu.bitcast(words << 16, F32)
    hi = pltpu.bitcast(words & jnp.uint32(0xFFFF0000), F32)
    return lo, hi


def _to_packed(src_ref, dst_ref, rows):
    def body(i, carry):
        r0 = pl.multiple_of(i * SUBLANES, SUBLANES)
        base = i * (SUBLANES * PACK_TILES)
        for c in range(PACK_TILES):
            lo = src_ref[pl.ds(r0, SUBLANES), c * LANES:(c + 1) * LANES]
            hi = src_ref[pl.ds(r0, SUBLANES), HALF_D + c * LANES:HALF_D + (c + 1) * LANES]
            dst_ref[pl.ds(base + c, SUBLANES, stride=PACK_TILES), :] = _pack_pair(lo, hi)
        return carry
    lax.fori_loop(0, rows // SUBLANES, body, 0)


def _norm_proj_kernel(x_ref, gain_ref, w_ref, cg_ref, cos_ref, sin_ref, bd_ref, *outs, sections, half):
    hb = (_rms_rows(x_ref[...]) * gain_ref[...]).astype(BF16)

    ys = [jnp.dot(hb, w_ref[:, c0:c0 + width], preferred_element_type=F32)
          for (c0, width, _, _, _) in sections]
    sss = []
    for y, (c0, width, kind, _, _) in zip(ys, sections):
        if kind == "plain":
            sss.append(None)
            continue
        y2 = y * y
        hi = y2.astype(BF16)
        lo = (y2 - hi.astype(F32)).astype(BF16)
        bd = bd_ref[:width, :width]
        sss.append(jnp.dot(hi, bd, preferred_element_type=F32)
                   + jnp.dot(lo, bd, preferred_element_type=F32))
    for y, ss, (c0, width, kind, oi, oc) in zip(ys, sss, sections):
        if kind != "plain":
            y = y * lax.rsqrt(ss + EPS) * cg_ref[:, c0:c0 + width]
        if kind == "rope":
            lane = lax.broadcasted_iota(jnp.int32, (1, width), 1)
            first = (lane % (2 * half)) < half
            part = jnp.where(first, pltpu.roll(y, width - half, 1), pltpu.roll(y, half, 1))
            y = y * cos_ref[:, :width] + part * sin_ref[:, :width]
        outs[oi][:, oc:oc + width] = y.astype(BF16)


def _norm_proj(x2d, gain, w_bf16, colgain, cos_t, sin_t, sections, out_widths, half, seq):
    m = x2d.shape[0]
    n_in = w_bf16.shape[1]
    bd = jnp.asarray(np.kron(np.eye(4), np.full((HEAD_DIM, HEAD_DIM), 1.0 / HEAD_DIM)), BF16)
    tiles_per_seq = seq // TM
    in_specs = [
        pl.BlockSpec((TM, D_MODEL), lambda i: (i, 0)),
        pl.BlockSpec((1, D_MODEL), lambda i: (0, 0)),
        pl.BlockSpec((D_MODEL, n_in), lambda i: (0, 0)),
        pl.BlockSpec((1, n_in), lambda i: (0, 0)),
        pl.BlockSpec((TM, 256), lambda i: (i % tiles_per_seq, 0)),
        pl.BlockSpec((TM, 256), lambda i: (i % tiles_per_seq, 0)),
        pl.BlockSpec((256, 256), lambda i: (0, 0)),
    ]
    out_shape = [jax.ShapeDtypeStruct((m, wd), BF16) for wd in out_widths]
    out_specs = [pl.BlockSpec((TM, wd), lambda i: (i, 0)) for wd in out_widths]
    kern = functools.partial(_norm_proj_kernel, sections=tuple(sections), half=half)
    return pl.pallas_call(
        kern, grid=(m // TM,), in_specs=in_specs, out_specs=out_specs, out_shape=out_shape,
        compiler_params=_cparams(("parallel",)), name="norm_proj")(
            x2d, gain.reshape(1, D_MODEL), w_bf16, colgain, cos_t, sin_t, bd)


def _na_kernel(q_ref, k_ref, v_ref, bias_ref, o_ref):
    nq = NA_QROWS * GRID_W
    nk = NA_KROWS * GRID_W
    rows = q_ref.shape[1] // GRID_W
    n_blocks = rows // NA_QROWS
    lane = lax.broadcasted_iota(jnp.int32, (1, LANES), 1)
    head0 = lane < HEAD_DIM

    for qb in range(n_blocks):
        r0 = qb * NA_QROWS
        start = min(max(r0 - NA_WIN_ROWS // 2, 0), rows - NA_KROWS)
        pat = 0 if qb == 0 else (2 if qb == n_blocks - 1 else 1)
        qs = r0 * GRID_W
        ks = start * GRID_W
        q = q_ref[0, qs:qs + nq, :]
        k = k_ref[0, ks:ks + nk, :]
        v = v_ref[0, ks:ks + nk, :]
        acc = None
        for hh in range(2):
            sel = head0 if hh == 0 else jnp.logical_not(head0)
            qm = jnp.where(sel, q, jnp.zeros_like(q))
            s = lax.dot_general(qm, k, (((1,), (1,)), ((), ())), preferred_element_type=F32)
            s = s + bias_ref[pat, hh]
            m = jnp.max(s, axis=-1, keepdims=True)
            p = jnp.exp2(s - m)
            l = jnp.sum(p, axis=-1, keepdims=True)
            o = jnp.dot(p.astype(BF16), v, preferred_element_type=F32) / l
            acc = o if acc is None else jnp.where(sel, o, acc)
        o_ref[0, qs:qs + nq, :] = acc.astype(BF16)


def _na_bias_tables(rel_bias, rows):
    n_dr, n_dc = 2 * NA_WIN_ROWS - 1, 2 * NA_WIN_COLS - 1
    sel_r = np.zeros((3, NA_QROWS, NA_KROWS, n_dr), np.float32)
    in_r = np.zeros((3, NA_QROWS, NA_KROWS), np.float32)
    for p, r0 in enumerate((0, 2 * NA_QROWS, rows - NA_QROWS)):
        start = int(np.clip(r0 - NA_WIN_ROWS // 2, 0, rows - NA_KROWS))
        for lr in range(NA_QROWS):
            r = r0 + lr
            rs = int(np.clip(r - NA_WIN_ROWS // 2, 0, rows - NA_WIN_ROWS))
            for kl in range(NA_KROWS):
                kr = start + kl
                if rs <= kr < rs + NA_WIN_ROWS:
                    in_r[p, lr, kl] = 1.0
                    sel_r[p, lr, kl, kr - r + NA_WIN_ROWS - 1] = 1.0
    sel_c = np.zeros((GRID_W, GRID_W, n_dc), np.float32)
    in_c = np.zeros((GRID_W, GRID_W), np.float32)
    for c in range(GRID_W):
        cs = int(np.clip(c - NA_WIN_COLS // 2, 0, GRID_W - NA_WIN_COLS))
        for kc in range(cs, cs + NA_WIN_COLS):
            in_c[c, kc] = 1.0
            dc = int(np.clip(kc - c, -(NA_WIN_COLS - 1), NA_WIN_COLS - 1))
            sel_c[c, kc, dc + NA_WIN_COLS - 1] = 1.0
    hp = lax.Precision.HIGHEST
    rows_sel = jnp.einsum('plkd,hde->phlke', jnp.asarray(sel_r), rel_bias.astype(F32), precision=hp)
    vals = jnp.einsum('phlke,cje->phlckj', rows_sel, jnp.asarray(sel_c), precision=hp)
    inside = jnp.asarray(in_r[:, None, :, None, :, None] * in_c[None, None, None, :, None, :])
    tab = jnp.where(inside > 0.5, vals * LOG2E, NEG_INF)
    return tab.reshape(3, rel_bias.shape[0], NA_QROWS * GRID_W, NA_KROWS * GRID_W)


def _na_attention(q, k, v, rel_bias):
    b, s, _ = q.shape
    rows = s // GRID_W
    assert rows % NA_QROWS == 0 and rows >= NA_KROWS and rows // NA_QROWS >= 3
    nq, nk = NA_QROWS * GRID_W, NA_KROWS * GRID_W
    bias = _na_bias_tables(rel_bias, rows)
    blk = pl.BlockSpec((1, s, LANES), lambda bi, hp: (bi, 0, hp))
    return pl.pallas_call(
        _na_kernel, grid=(b, A_HEADS // 2),
        in_specs=[blk, blk, blk, pl.BlockSpec((3, 2, nq, nk), lambda bi, hp: (0, hp, 0, 0))],
        out_specs=blk, out_shape=jax.ShapeDtypeStruct((b, s, A_W), BF16),
        compiler_params=_cparams(("parallel", "arbitrary")), name="na_attention")(q, k, v, bias)


def _swa_kernel(sink_ref, q_ref, k_ref, v_ref, rep_ref, o_ref, kt_ref, vt_ref):
    s_len = q_ref.shape[1]
    n_groups = B_HEADS // B_KV_HEADS
    gw = n_groups * HEAD_DIM
    lane = lax.broadcasted_iota(jnp.int32, (1, gw), 1)
    rel = (lax.broadcasted_iota(jnp.int32, (SWA_Q, SWA_K), 0)
           - lax.broadcasted_iota(jnp.int32, (SWA_Q, SWA_K), 1))

    for g in range(B_KV_HEADS):
        kt_ref[...] = jnp.dot(k_ref[0], rep_ref[g], preferred_element_type=F32).astype(BF16)
        vt_ref[...] = jnp.dot(v_ref[0], rep_ref[g], preferred_element_type=F32).astype(BF16)

        for i in range(s_len // SWA_Q):
            q0 = i * SWA_Q
            k0 = min(max(q0 - B_WINDOW, 0), s_len - SWA_K)
            q = q_ref[0, q0:q0 + SWA_Q, g * gw:(g + 1) * gw]
            kt = kt_ref[k0:k0 + SWA_K, :]
            vt = vt_ref[k0:k0 + SWA_K, :]
            valid = jnp.abs(rel + (q0 - k0)) <= B_WINDOW
            acc = jnp.zeros((SWA_Q, gw), F32)
            for j in range(n_groups):
                sel = (lane >= j * HEAD_DIM) & (lane < (j + 1) * HEAD_DIM)
                sink = sink_ref[g * n_groups + j]
                qm = jnp.where(sel, q, jnp.zeros_like(q))
                s = lax.dot_general(qm, kt, (((1,), (1,)), ((), ())), preferred_element_type=F32)
                s = jnp.where(valid, s, NEG_INF)
                m = jnp.maximum(jnp.max(s, axis=-1, keepdims=True), sink)
                e = jnp.exp2(s - m)
                den = jnp.sum(e, axis=-1, keepdims=True) + jnp.exp2(sink - m)
                o = jnp.dot(e.astype(BF16), vt, preferred_element_type=F32) / den
                acc = jnp.where(sel, o, acc)
            o_ref[0, q0:q0 + SWA_Q, g * gw:(g + 1) * gw] = acc.astype(BF16)


def _replication_matrices(n_kv):
    rep = np.zeros((n_kv, n_kv * HEAD_DIM, 4 * HEAD_DIM), np.float32)
    for g in range(n_kv):
        for j in range(4 * HEAD_DIM):
            rep[g, g * HEAD_DIM + j % HEAD_DIM, j] = 1.0
    return jnp.asarray(rep, BF16)


def _swa_attention(q, k, v, sink):
    b, s, _ = q.shape
    assert s % SWA_Q == 0 and s >= SWA_K and SWA_K >= SWA_Q + 2 * B_WINDOW
    rep = _replication_matrices(B_KV_HEADS)
    return pl.pallas_call(
        _swa_kernel, grid=(b,),
        in_specs=[pl.BlockSpec(memory_space=pltpu.SMEM),
                  pl.BlockSpec((1, s, B_QW), lambda bi: (bi, 0, 0)),
                  pl.BlockSpec((1, s, B_KVW), lambda bi: (bi, 0, 0)),
                  pl.BlockSpec((1, s, B_KVW), lambda bi: (bi, 0, 0)),
                  pl.BlockSpec((B_KV_HEADS, B_KVW, 4 * HEAD_DIM), lambda bi: (0, 0, 0))],
        out_specs=pl.BlockSpec((1, s, B_QW), lambda bi: (bi, 0, 0)),
        out_shape=jax.ShapeDtypeStruct((b, s, B_QW), BF16),
        scratch_shapes=[pltpu.VMEM((s, 4 * HEAD_DIM), BF16), pltpu.VMEM((s, 4 * HEAD_DIM), BF16)],
        compiler_params=_cparams(("parallel",)), name="swa_attention")(sink.astype(F32) * LOG2E, q, k, v, rep)


def _dense_kernel(q_ref, k_ref, v_ref, rep_ref, o_ref, kt_ref, vt_ref):
    s_len = q_ref.shape[1]
    gw = 4 * HEAD_DIM
    lane = lax.broadcasted_iota(jnp.int32, (1, gw), 1)
    kt_ref[...] = jnp.dot(k_ref[0], rep_ref[0], preferred_element_type=F32).astype(BF16)
    vt_ref[...] = jnp.dot(v_ref[0], rep_ref[0], preferred_element_type=F32).astype(BF16)

    def body(i, carry):
        for u in range(DENSE_UNROLL):
            q0 = pl.multiple_of((i * DENSE_UNROLL + u) * DENSE_Q, DENSE_Q)
            q = q_ref[0, pl.ds(q0, DENSE_Q), :]
            acc = jnp.zeros((DENSE_Q, gw), F32)
            for j in range(4):
                sel = (lane >= j * HEAD_DIM) & (lane < (j + 1) * HEAD_DIM)
                qm = jnp.where(sel, q, jnp.zeros_like(q))
                s = lax.dot_general(qm, kt_ref[...], (((1,), (1,)), ((), ())), preferred_element_type=F32)
                m = jnp.max(s, axis=-1, keepdims=True)
                p = jnp.exp2(s - m)
                l = jnp.sum(p, axis=-1, keepdims=True)
                o = jnp.dot(p.astype(BF16), vt_ref[...], preferred_element_type=F32) / l
                acc = jnp.where(sel, o, acc)
            o_ref[0, pl.ds(q0, DENSE_Q), :] = acc.astype(BF16)
        return carry

    lax.fori_loop(0, s_len // (DENSE_Q * DENSE_UNROLL), body, 0)


def _dense_attention(q, k, v):
    b, s, _ = q.shape
    gw = 4 * HEAD_DIM
    rep = _replication_matrices(C_KV_HEADS)
    return pl.pallas_call(
        _dense_kernel, grid=(b, C_KV_HEADS),
        in_specs=[pl.BlockSpec((1, s, gw), lambda bi, g: (bi, 0, g)),
                  pl.BlockSpec((1, s, C_KVW), lambda bi, g: (bi, 0, 0)),
                  pl.BlockSpec((1, s, C_KVW), lambda bi, g: (bi, 0, 0)),
                  pl.BlockSpec((1, C_KVW, gw), lambda bi, g: (g, 0, 0))],
        out_specs=pl.BlockSpec((1, s, gw), lambda bi, g: (bi, 0, g)),
        out_shape=jax.ShapeDtypeStruct((b, s, C_QW), BF16),
        scratch_shapes=[pltpu.VMEM((s, gw), BF16), pltpu.VMEM((s, gw), BF16)],
        compiler_params=_cparams(("parallel", "arbitrary")), name="dense_attention")(q, k, v, rep)


def _out_proj_kernel(*refs, n_parts):
    it = iter(refs)
    x_ref = next(it)
    parts = [next(it) for _ in range(n_parts)]
    w_ref = next(it)
    gain_ref = next(it)
    wr_ref = next(it)
    xo_ref = next(it)
    hpk_ref = next(it)
    aff_ref = next(it)
    afft_ref = next(it)
    h_ref = next(it)

    y = x_ref[...]
    c0 = 0
    for p_ref in parts:
        wp = p_ref.shape[1]
        y = y + jnp.dot(p_ref[...], w_ref[c0:c0 + wp, :], preferred_element_type=F32)
        c0 += wp
    xo_ref[...] = y
    h = _rms_rows(y) * gain_ref[...]
    h_ref[...] = h
    _to_packed(h_ref, hpk_ref, TM)

    h_hi = h.astype(BF16)
    h_lo = (h - h_hi.astype(F32)).astype(BF16)
    r_hi = jnp.dot(h_hi, wr_ref[...], preferred_element_type=F32)
    r_lo = jnp.dot(h_lo, wr_ref[:, :LANES], preferred_element_type=F32)
    lane = lax.broadcasted_iota(jnp.int32, (1, LANES), 1)
    logits = r_hi[:, :LANES] + r_hi[:, LANES:] + r_lo
    logits = jnp.where(lane < N_EXPERTS, logits, NEG_INF)
    e = jnp.exp(logits - jnp.max(logits, axis=-1, keepdims=True))
    aff = e / jnp.sum(e, axis=-1, keepdims=True)
    aff_ref[...] = aff
    afft_ref[0] = jnp.transpose(aff)[:N_EXPERTS, :]


def _out_proj(x2d, parts, w_out_bf16, ffn_gain, w_router, batch, seq):
    m = x2d.shape[0]
    tiles_per_seq = seq // TM
    wr = w_router.astype(F32)
    wr_hi = wr.astype(BF16)
    wr_lo = (wr - wr_hi.astype(F32)).astype(BF16)
    wr_split = jnp.zeros((D_MODEL, 2 * LANES), BF16)
    wr_split = wr_split.at[:, :N_EXPERTS].set(wr_hi).at[:, LANES:LANES + N_EXPERTS].set(wr_lo)
    in_specs = [pl.BlockSpec((TM, D_MODEL), lambda i: (i, 0))]
    in_specs += [pl.BlockSpec((TM, p.shape[1]), lambda i: (i, 0)) for p in parts]
    in_specs += [pl.BlockSpec((D_MODEL, D_MODEL), lambda i: (0, 0)),
                 pl.BlockSpec((1, D_MODEL), lambda i: (0, 0)),
                 pl.BlockSpec((D_MODEL, 2 * LANES), lambda i: (0, 0))]
    out_shape = [jax.ShapeDtypeStruct((m, D_MODEL), F32),
                 jax.ShapeDtypeStruct((m * PACK_TILES, LANES), U32),
                 jax.ShapeDtypeStruct((m, LANES), F32),
                 jax.ShapeDtypeStruct((batch, N_EXPERTS, seq), F32)]
    out_specs = [pl.BlockSpec((TM, D_MODEL), lambda i: (i, 0)),
                 pl.BlockSpec((TM * PACK_TILES, LANES), lambda i: (i, 0)),
                 pl.BlockSpec((TM, LANES), lambda i: (i, 0)),
                 pl.BlockSpec((1, N_EXPERTS, TM), lambda i: (i // tiles_per_seq, 0, i % tiles_per_seq))]
    kern = functools.partial(_out_proj_kernel, n_parts=len(parts))
    return pl.pallas_call(
        kern, grid=(m // TM,), in_specs=in_specs, out_specs=out_specs, out_shape=out_shape,
        scratch_shapes=[pltpu.VMEM((TM, D_MODEL), F32)],
        compiler_params=_cparams(("parallel",)), name="out_proj_router")(
            x2d, *parts, w_out_bf16, ffn_gain.reshape(1, D_MODEL), wr_split)


def _topk_kernel(afft_ref, aff_ref, ids_ref, gate_ref, res_ref, *, cap):
    s_len = afft_ref.shape[2]
    at = afft_ref[0]
    bits = pltpu.bitcast(at, jnp.int32)

    def count(mask):
        return jnp.sum(jnp.where(mask, 1.0, 0.0), axis=1, keepdims=True)

    def bit_step(k, thr):
        cand = thr | jnp.left_shift(jnp.int32(1), 30 - k)
        return jnp.where(count(bits >= cand) >= cap, cand, thr)

    thr = lax.fori_loop(0, 31, bit_step, jnp.zeros((N_EXPERTS, 1), jnp.int32))
    gt = bits > thr
    eq = bits == thr
    need = cap - count(gt)

    ri = lax.broadcasted_iota(jnp.int32, (LANES, LANES), 0)
    ci = lax.broadcasted_iota(jnp.int32, (LANES, LANES), 1)
    upper = jnp.where(ri < ci, 1.0, 0.0).astype(BF16)

    def prefix(mask_f32):
        out = []
        run = jnp.zeros((N_EXPERTS, 1), F32)
        for c in range(s_len // LANES):
            blk = mask_f32[:, c * LANES:(c + 1) * LANES]
            out.append(jnp.dot(blk.astype(BF16), upper, preferred_element_type=F32) + run)
            run = run + jnp.sum(blk, axis=1, keepdims=True)
        return out

    eq_f = jnp.where(eq, 1.0, 0.0)
    eq_rank = prefix(eq_f)
    sel_tiles = []
    for c in range(s_len // LANES):
        sl = slice(c * LANES, (c + 1) * LANES)
        sel_tiles.append(jnp.where(gt[:, sl] | (eq[:, sl] & (eq_rank[c] < need)), 1.0, 0.0))
    sel_f = jnp.concatenate(sel_tiles, axis=1)
    pos = jnp.concatenate(prefix(sel_f), axis=1)
    key = jnp.where(sel_f > 0.5, pos, -1.0)

    a = aff_ref[0]
    a_hi = a.astype(BF16).astype(F32)
    r1 = a - a_hi
    a_mid = r1.astype(BF16).astype(F32)
    a_lo = r1 - a_mid
    lane = lax.broadcasted_iota(jnp.int32, (s_len, LANES), 1)
    tok = lax.broadcasted_iota(jnp.int32, (s_len, LANES), 0)
    packed = a_hi + pltpu.roll(a_mid, N_EXPERTS, 1) + pltpu.roll(a_lo, 2 * N_EXPERTS, 1)
    packed = jnp.where(lane == 64, (tok // 64).astype(F32), packed)
    packed = jnp.where(lane == 65, (tok % 64).astype(F32), packed)
    rmat = packed.astype(BF16)

    slot = lax.broadcasted_iota(jnp.int32, (cap, 1), 0).astype(F32)
    for e in range(N_EXPERTS):
        onehot_t = jnp.where(key[e:e + 1, :] == slot, 1.0, 0.0).astype(BF16)
        res_ref[...] = jnp.dot(onehot_t, rmat, preferred_element_type=F32)
        res = jnp.transpose(res_ref[...])
        ids_ref[0, e:e + 1, :] = (res[64:65, :] * 64.0 + res[65:66, :]).astype(jnp.int32)
        gate_ref[0, e:e + 1, :] = (res[e:e + 1, :] + res[N_EXPERTS + e:N_EXPERTS + e + 1, :]
                                   + res[2 * N_EXPERTS + e:2 * N_EXPERTS + e + 1, :])


def _topk(aff_t, aff, cap):
    b, _, s = aff_t.shape
    kern = functools.partial(_topk_kernel, cap=cap)
    return pl.pallas_call(
        kern, grid=(b,),
        in_specs=[pl.BlockSpec((1, N_EXPERTS, s), lambda bi: (bi, 0, 0)),
                  pl.BlockSpec((1, s, LANES), lambda bi: (bi, 0, 0))],
        out_specs=[pl.BlockSpec((1, N_EXPERTS, cap), lambda bi: (bi, 0, 0)),
                   pl.BlockSpec((1, N_EXPERTS, cap), lambda bi: (bi, 0, 0))],
        out_shape=[jax.ShapeDtypeStruct((b, N_EXPERTS, cap), jnp.int32),
                   jax.ShapeDtypeStruct((b, N_EXPERTS, cap), F32)],
        scratch_shapes=[pltpu.VMEM((cap, LANES), F32)],
        compiler_params=_cparams(("parallel",)), name="expert_topk")(aff_t, aff)


MOE_GROUP = 16
SCATTER_GROUP = 8
MERGE_ROWS = 512


MOE_EXPERTS_PER_STEP = 4
FFN_ROWS = 512
FFN_F_CHUNK = 512


def _gather_kernel(ids_ref, h_ref, o_ref, *, cap):
    for el in range(MOE_EXPERTS_PER_STEP):
        def body(jg, carry, el=el):
            for u in range(MOE_GROUP):
                j = jg * MOE_GROUP + u
                o_ref[el, j] = h_ref[0, ids_ref[0, 0, el * cap + j]]
            return carry
        lax.fori_loop(0, cap // MOE_GROUP, body, 0)


def _ffn_kernel(xe_ref, wg_ref, wu_ref, wd_ref, o_ref, xb_ref):
    pair = 2 * SUBLANES
    for i in range(FFN_ROWS // pair):
        for c in range(PACK_TILES):
            base = i * pair * PACK_TILES + c
            lo_a, hi_a = _unpack_pair(xe_ref[0, pl.ds(base, SUBLANES, stride=PACK_TILES), :])
            lo_b, hi_b = _unpack_pair(
                xe_ref[0, pl.ds(base + SUBLANES * PACK_TILES, SUBLANES, stride=PACK_TILES), :])
            rows = slice(i * pair, (i + 1) * pair)
            xb_ref[rows, c * LANES:(c + 1) * LANES] = jnp.concatenate([lo_a, lo_b], axis=0).astype(BF16)
            xb_ref[rows, HALF_D + c * LANES:HALF_D + (c + 1) * LANES] = (
                jnp.concatenate([hi_a, hi_b], axis=0).astype(BF16))
    xe = xb_ref[...]
    y = None
    for f0 in range(0, D_FF, FFN_F_CHUNK):
        hg = jnp.dot(xe, wg_ref[0, 0, :, f0:f0 + FFN_F_CHUNK], preferred_element_type=F32)
        hu = jnp.dot(xe, wu_ref[0, 0, :, f0:f0 + FFN_F_CHUNK], preferred_element_type=F32)
        hid = (hg * (1.0 / (1.0 + jnp.exp(-hg))) * hu).astype(BF16)
        part = jnp.dot(hid, wd_ref[0, 0, f0:f0 + FFN_F_CHUNK, :], preferred_element_type=F32)
        y = part if y is None else y + part
    for i in range(FFN_ROWS // SUBLANES):
        for c in range(ROW_TILES):
            o_ref[0, pl.ds(i * SUBLANES * ROW_TILES + c, SUBLANES, stride=ROW_TILES), :] = (
                y[i * SUBLANES:(i + 1) * SUBLANES, c * LANES:(c + 1) * LANES])


def _scatter_kernel(ids_ref, gate_ref, y_ref, x_ref, o_ref, acc_ref, *, cap, n_eg):
    step = pl.program_id(1)

    @pl.when(step == 0)
    def _():
        acc_ref[...] = jnp.zeros_like(acc_ref)

    @pl.when(step < n_eg)
    def _():
        for el in range(MOE_EXPERTS_PER_STEP):
            def body(jg, carry, el=el):
                rows = []
                for u in range(SCATTER_GROUP):
                    j = jg * SCATTER_GROUP + u
                    rows.append(pl.multiple_of(ids_ref[0, 0, el * cap + j] * ROW_TILES, ROW_TILES))
                vals = []
                for u in range(SCATTER_GROUP):
                    j = jg * SCATTER_GROUP + u
                    src = pl.multiple_of(j * ROW_TILES, ROW_TILES)
                    gate = gate_ref[0, 0, el * cap + j]
                    vals.append(acc_ref[pl.ds(rows[u], ROW_TILES), :]
                                + y_ref[el, pl.ds(src, ROW_TILES), :] * gate)
                for u in range(SCATTER_GROUP):
                    acc_ref[pl.ds(rows[u], ROW_TILES), :] = vals[u]
                return carry
            lax.fori_loop(0, cap // SCATTER_GROUP, body, 0)

    @pl.when(step >= n_eg)
    def _():
        first = (step - n_eg) * (MERGE_ROWS * ROW_TILES)

        def body(i, carry):
            r0 = pl.multiple_of(i * SUBLANES, SUBLANES)
            base = first + i * (SUBLANES * ROW_TILES)
            for c in range(ROW_TILES):
                cols = slice(c * LANES, (c + 1) * LANES)
                o_ref[pl.ds(r0, SUBLANES), cols] = (
                    x_ref[pl.ds(r0, SUBLANES), cols]
                    + acc_ref[pl.ds(base + c, SUBLANES, stride=ROW_TILES), :])
            return carry
        lax.fori_loop(0, MERGE_ROWS // SUBLANES, body, 0)


def _moe(x2d, ids, gates, h_pk, wg, wu, wd, layer, cap):
    b = ids.shape[0]
    seq = h_pk.shape[0] // (b * PACK_TILES)
    n_eg = N_EXPERTS // MOE_EXPERTS_PER_STEP
    step_ids = MOE_EXPERTS_PER_STEP * cap
    ids_g = ids.reshape(b * n_eg, 1, step_ids)
    gates_g = gates.reshape(b * n_eg, 1, step_ids)
    smem_spec = pl.BlockSpec((1, 1, step_ids), lambda bi, eg: (bi * n_eg + eg, 0, 0), memory_space=pltpu.SMEM)

    xe_pk = pl.pallas_call(
        functools.partial(_gather_kernel, cap=cap), grid=(b, n_eg),
        in_specs=[smem_spec, pl.BlockSpec((1, seq, PACK_TILES, LANES), lambda bi, eg: (bi, 0, 0, 0))],
        out_specs=pl.BlockSpec((MOE_EXPERTS_PER_STEP, cap, PACK_TILES, LANES), lambda bi, eg: (eg, bi, 0, 0)),
        out_shape=jax.ShapeDtypeStruct((N_EXPERTS, b * cap, PACK_TILES, LANES), U32),
        compiler_params=_cparams(("parallel", "arbitrary")), name="moe_gather")(
            ids_g, h_pk.reshape(b, seq, PACK_TILES, LANES))

    assert (b * cap) % FFN_ROWS == 0
    slot_rows = cap * ROW_TILES
    ye_rt = pl.pallas_call(
        _ffn_kernel, grid=(N_EXPERTS, b * cap // FFN_ROWS),
        in_specs=[pl.BlockSpec((1, FFN_ROWS * PACK_TILES, LANES), lambda e, i: (e, i, 0)),
                  pl.BlockSpec((1, 1, D_MODEL, D_FF), lambda e, i: (layer, e, 0, 0)),
                  pl.BlockSpec((1, 1, D_MODEL, D_FF), lambda e, i: (layer, e, 0, 0)),
                  pl.BlockSpec((1, 1, D_FF, D_MODEL), lambda e, i: (layer, e, 0, 0))],
        out_specs=pl.BlockSpec((1, FFN_ROWS * ROW_TILES, LANES), lambda e, i: (e, i, 0)),
        out_shape=jax.ShapeDtypeStruct((N_EXPERTS, b * slot_rows, LANES), F32),
        scratch_shapes=[pltpu.VMEM((FFN_ROWS, D_MODEL), BF16)],
        compiler_params=_cparams(("parallel", "arbitrary")), name="moe_ffn")(
            xe_pk.reshape(N_EXPERTS, b * cap * PACK_TILES, LANES), wg, wu, wd)

    assert seq % MERGE_ROWS == 0
    n_merge = seq // MERGE_ROWS

    def eg_of(st):
        return jnp.minimum(st, n_eg - 1)

    def rows_of(bi, st):
        return (bi * n_merge + jnp.maximum(st - n_eg, 0), 0)

    smem_step = pl.BlockSpec((1, 1, step_ids), lambda bi, st: (bi * n_eg + eg_of(st), 0, 0),
                             memory_space=pltpu.SMEM)
    return pl.pallas_call(
        functools.partial(_scatter_kernel, cap=cap, n_eg=n_eg), grid=(b, n_eg + n_merge),
        in_specs=[smem_step, smem_step,
                  pl.BlockSpec((MOE_EXPERTS_PER_STEP, slot_rows, LANES), lambda bi, st: (eg_of(st), bi, 0)),
                  pl.BlockSpec((MERGE_ROWS, D_MODEL), rows_of)],
        out_specs=pl.BlockSpec((MERGE_ROWS, D_MODEL), rows_of),
        out_shape=jax.ShapeDtypeStruct((b * seq, D_MODEL), F32),
        scratch_shapes=[pltpu.VMEM((seq * ROW_TILES, LANES), F32)],
        compiler_params=_cparams(("parallel", "arbitrary")), name="moe_scatter_merge")(
            ids_g, gates_g, ye_rt, x2d)


def _rope_tables(seq):
    t = jnp.arange(seq).astype(F32)
    inv = jnp.power(jnp.float32(ROPE_THETA), -jnp.arange(0, HEAD_DIM, 2, dtype=F32) / HEAD_DIM)
    ang = t[:, None] * inv[None, :]
    cos = jnp.concatenate([jnp.cos(ang), jnp.cos(ang)], axis=1)
    sin = jnp.concatenate([-jnp.sin(ang), jnp.sin(ang)], axis=1)
    return jnp.tile(cos, (1, 4)), jnp.tile(sin, (1, 4))


def _axial_tables(seq):
    t = jnp.arange(seq)
    half = HEAD_DIM // 2
    inv = jnp.power(jnp.float32(ROPE_THETA), -jnp.arange(0, half, 2, dtype=F32) / half)
    ang_r = (t // GRID_W).astype(F32)[:, None] * inv[None, :]
    ang_c = (t % GRID_W).astype(F32)[:, None] * inv[None, :]
    cos = jnp.concatenate([jnp.cos(ang_r), jnp.cos(ang_r), jnp.cos(ang_c), jnp.cos(ang_c)], axis=1)
    sin = jnp.concatenate([-jnp.sin(ang_r), jnp.sin(ang_r), -jnp.sin(ang_c), jnp.sin(ang_c)], axis=1)
    return jnp.tile(cos, (1, 4)), jnp.tile(sin, (1, 4))


def _col_gain(pieces):
    cols = []
    for gain, n_heads, scale in pieces:
        if gain is None:
            cols.append(jnp.ones((n_heads * HEAD_DIM,), F32))
        else:
            cols.append(jnp.tile(gain.astype(F32) * scale, n_heads))
    return jnp.concatenate(cols)[None, :]


def _chunks(start, width, kind, oi):
    out = []
    step = 256 if width % 256 == 0 else LANES
    for c in range(0, width, step):
        out.append((start + c, step, kind, oi, c))
    return out


def _moe_block(x2d, parts, w_out, ffn_gain, w_router, wg, wu, wd, layer, batch, seq):
    cap = EC_CAPACITY_FACTOR * seq // N_EXPERTS
    x_new, h_pk, aff, aff_t = _out_proj(x2d, parts, w_out.astype(BF16), ffn_gain, w_router, batch, seq)
    ids, gates = _topk(aff_t, aff.reshape(batch, seq, LANES), cap)
    return _moe(x_new, ids, gates, h_pk, wg, wu, wd, layer, cap)


def kernel(x, attn_norm_even, w_in_even, q_norm_a, k_norm_a, rel_bias_a, q_norm_b, k_norm_b, sink_b,
           w_out_even, attn_norm_odd, w_in_odd, q_norm_c, k_norm_c, w_out_odd,
           ffn_norm, w_router, w_gate, w_up, w_down):
    batch, seq, _ = x.shape
    m = batch * seq
    scale = HEAD_DIM ** -0.5 * LOG2E
    x2d = x.reshape(m, D_MODEL)
    depth = ffn_norm.shape[0]
    wg_all, wu_all, wd_all = w_gate.astype(BF16), w_up.astype(BF16), w_down.astype(BF16)
    for layer in range(depth):
        i = layer // 2
        if layer % 2 == 0:
            cos_t, sin_t = _rope_tables(seq)
            colgain = _col_gain([(q_norm_a[i], A_HEADS, scale), (k_norm_a[i], A_HEADS, 1.0),
                                 (None, A_HEADS, 1.0), (q_norm_b[i], B_HEADS, scale),
                                 (k_norm_b[i], B_KV_HEADS, 1.0), (None, B_KV_HEADS, 1.0)])
            sections = (_chunks(0, A_W, "norm", 0) + _chunks(A_W, A_W, "norm", 1)
                        + _chunks(2 * A_W, A_W, "plain", 2) + _chunks(3 * A_W, B_QW, "rope", 3)
                        + _chunks(3 * A_W + B_QW, B_KVW, "rope", 4)
                        + _chunks(3 * A_W + B_QW + B_KVW, B_KVW, "plain", 5))
            widths = [A_W, A_W, A_W, B_QW, B_KVW, B_KVW]
            outs = _norm_proj(x2d, attn_norm_even[i], w_in_even[i].astype(BF16), colgain,
                              cos_t, sin_t, sections, widths, HEAD_DIM // 2, seq)
            qa, ka, va, qb, kb, vb = [o.reshape(batch, seq, -1) for o in outs]
            out_a = _na_attention(qa, ka, va, rel_bias_a[i])
            out_b = _swa_attention(qb, kb, vb, sink_b[i])
            parts = [out_a.reshape(m, A_W), out_b.reshape(m, B_QW)]
            w_out = w_out_even[i]
        else:
            cos_t, sin_t = _axial_tables(seq)
            colgain = _col_gain([(q_norm_c[i], C_HEADS, scale), (k_norm_c[i], C_KV_HEADS, 1.0),
                                 (None, C_KV_HEADS, 1.0)])
            sections = (_chunks(0, C_QW, "rope", 0) + _chunks(C_QW, C_KVW, "rope", 1)
                        + _chunks(C_QW + C_KVW, C_KVW, "plain", 2))
            widths = [C_QW, C_KVW, C_KVW]
            outs = _norm_proj(x2d, attn_norm_odd[i], w_in_odd[i].astype(BF16), colgain,
                              cos_t, sin_t, sections, widths, HEAD_DIM // 4, seq)
            qc, kc, vc = [o.reshape(batch, seq, -1) for o in outs]
            parts = [_dense_attention(qc, kc, vc).reshape(m, C_QW)]
            w_out = w_out_odd[i]
        x2d = _moe_block(x2d, parts, w_out, ffn_norm[layer], w_router[layer],
                         wg_all, wu_all, wd_all, layer, batch, seq)
    return x2d.reshape(batch, seq, D_MODEL)
```

```python
import functools

import numpy as np
import jax
import jax.numpy as jnp
from jax import lax
from jax.experimental import pallas as pl
from jax.experimental.pallas import tpu as pltpu

D_MODEL = 1024
HEAD_DIM = 64
GRID_W = 64
ROPE_THETA = 10000.0
EPS = 1e-6
NEG_INF = -1e30
LOG2E = 1.4426950408889634

A_HEADS = 8
NA_WIN_ROWS = 8
NA_WIN_COLS = 16
B_HEADS = 8
B_KV_HEADS = 2
B_WINDOW = 128
C_HEADS = 16
C_KV_HEADS = 4
N_EXPERTS = 16
EC_CAPACITY_FACTOR = 2
D_FF = 2048

A_W = A_HEADS * HEAD_DIM
B_QW = B_HEADS * HEAD_DIM
B_KVW = B_KV_HEADS * HEAD_DIM
C_QW = C_HEADS * HEAD_DIM
C_KVW = C_KV_HEADS * HEAD_DIM

SUBLANES = 8
LANES = 128
ROW_TILES = D_MODEL // LANES
VMEM_LIMIT = 56 * 1024 * 1024

TM = 512
NA_QROWS = 4
NA_KROWS = 12
SWA_Q = 256
SWA_K = 512
DENSE_Q = 256
DENSE_UNROLL = 8

F32 = jnp.float32
BF16 = jnp.bfloat16


def _cparams(sem):
    return pltpu.CompilerParams(dimension_semantics=sem, vmem_limit_bytes=VMEM_LIMIT)


def _rms_rows(x):
    return x * lax.rsqrt(jnp.mean(x * x, axis=-1, keepdims=True) + EPS)


PACK_TILES = ROW_TILES // 2
HALF_D = D_MODEL // 2
U32 = jnp.uint32


def _pack_pair(lo, hi):
    lo_bits = pltpu.bitcast(lo.astype(BF16).astype(F32), U32) >> 16
    hi_bits = pltpu.bitcast(hi.astype(BF16).astype(F32), U32) & jnp.uint32(0xFFFF0000)
    return lo_bits | hi_bits


def _unpack_pair(words):
    lo = pltpu.bitcast(words << 16, F32)
    hi = pltpu.bitcast(words & jnp.uint32(0xFFFF0000), F32)
    return lo, hi


def _to_packed(src_ref, dst_ref, rows):
    def body(i, carry):
        r0 = pl.multiple_of(i * SUBLANES, SUBLANES)
        base = i * (SUBLANES * PACK_TILES)
        for c in range(PACK_TILES):
            lo = src_ref[pl.ds(r0, SUBLANES), c * LANES:(c + 1) * LANES]
            hi = src_ref[pl.ds(r0, SUBLANES), HALF_D + c * LANES:HALF_D + (c + 1) * LANES]
            dst_ref[pl.ds(base + c, SUBLANES, stride=PACK_TILES), :] = _pack_pair(lo, hi)
        return carry
    lax.fori_loop(0, rows // SUBLANES, body, 0)


def _norm_proj_kernel(x_ref, gain_ref, w_ref, cg_ref, cos_ref, sin_ref, bd_ref, *outs, sections, half):
    hb = (_rms_rows(x_ref[...]) * gain_ref[...]).astype(BF16)

    ys = [jnp.dot(hb, w_ref[:, c0:c0 + width], preferred_element_type=F32)
          for (c0, width, _, _, _) in sections]
    sss = []
    for y, (c0, width, kind, _, _) in zip(ys, sections):
        if kind == "plain":
            sss.append(None)
            continue
        y2 = y * y
        hi = y2.astype(BF16)
        lo = (y2 - hi.astype(F32)).astype(BF16)
        bd = bd_ref[:width, :width]
        sss.append(jnp.dot(hi, bd, preferred_element_type=F32)
                   + jnp.dot(lo, bd, preferred_element_type=F32))
    for y, ss, (c0, width, kind, oi, oc) in zip(ys, sss, sections):
        if kind != "plain":
            y = y * lax.rsqrt(ss + EPS) * cg_ref[:, c0:c0 + width]
        if kind == "rope":
            lane = lax.broadcasted_iota(jnp.int32, (1, width), 1)
            first = (lane % (2 * half)) < half
            part = jnp.where(first, pltpu.roll(y, width - half, 1), pltpu.roll(y, half, 1))
            y = y * cos_ref[:, :width] + part * sin_ref[:, :width]
        outs[oi][:, oc:oc + width] = y.astype(BF16)


def _norm_proj(x2d, gain, w_bf16, colgain, cos_t, sin_t, sections, out_widths, half, seq):
    m = x2d.shape[0]
    n_in = w_bf16.shape[1]
    bd = jnp.asarray(np.kron(np.eye(4), np.full((HEAD_DIM, HEAD_DIM), 1.0 / HEAD_DIM)), BF16)
    tiles_per_seq = seq // TM
    in_specs = [
        pl.BlockSpec((TM, D_MODEL), lambda i: (i, 0)),
        pl.BlockSpec((1, D_MODEL), lambda i: (0, 0)),
        pl.BlockSpec((D_MODEL, n_in), lambda i: (0, 0)),
        pl.BlockSpec((1, n_in), lambda i: (0, 0)),
        pl.BlockSpec((TM, 256), lambda i: (i % tiles_per_seq, 0)),
        pl.BlockSpec((TM, 256), lambda i: (i % tiles_per_seq, 0)),
        pl.BlockSpec((256, 256), lambda i: (0, 0)),
    ]
    out_shape = [jax.ShapeDtypeStruct((m, wd), BF16) for wd in out_widths]
    out_specs = [pl.BlockSpec((TM, wd), lambda i: (i, 0)) for wd in out_widths]
    kern = functools.partial(_norm_proj_kernel, sections=tuple(sections), half=half)
    return pl.pallas_call(
        kern, grid=(m // TM,), in_specs=in_specs, out_specs=out_specs, out_shape=out_shape,
        compiler_params=_cparams(("parallel",)), name="norm_proj")(
            x2d, gain.reshape(1, D_MODEL), w_bf16, colgain, cos_t, sin_t, bd)


def _na_kernel(q_ref, k_ref, v_ref, bias_ref, o_ref):
    nq = NA_QROWS * GRID_W
    nk = NA_KROWS * GRID_W
    rows = q_ref.shape[1] // GRID_W
    n_blocks = rows // NA_QROWS
    lane = lax.broadcasted_iota(jnp.int32, (1, LANES), 1)
    head0 = lane < HEAD_DIM

    edge_keys = NA_WIN_ROWS * GRID_W
    for qb in range(n_blocks):
        r0 = qb * NA_QROWS
        start = min(max(r0 - NA_WIN_ROWS // 2, 0), rows - NA_KROWS)
        pat = 0 if qb == 0 else (2 if qb == n_blocks - 1 else 1)
        c0, c1 = {0: (0, edge_keys), 1: (0, nk), 2: (nk - edge_keys, nk)}[pat]
        qs = r0 * GRID_W
        ks = start * GRID_W + c0
        q = q_ref[0, qs:qs + nq, :]
        k = k_ref[0, ks:ks + (c1 - c0), :]
        v = v_ref[0, ks:ks + (c1 - c0), :]
        acc = None
        for hh in range(2):
            sel = head0 if hh == 0 else jnp.logical_not(head0)
            qm = jnp.where(sel, q, jnp.zeros_like(q))
            s = lax.dot_general(qm, k, (((1,), (1,)), ((), ())), preferred_element_type=F32)
            s = s + bias_ref[pat, hh, :, c0:c1]
            m = jnp.max(s, axis=-1, keepdims=True)
            p = jnp.exp2(s - m)
            l = jnp.sum(p, axis=-1, keepdims=True)
            o = jnp.dot(p.astype(BF16), v, preferred_element_type=F32) / l
            acc = o if acc is None else jnp.where(sel, o, acc)
        o_ref[0, qs:qs + nq, :] = acc.astype(BF16)


def _na_bias_tables(rel_bias, rows):
    n_dr, n_dc = 2 * NA_WIN_ROWS - 1, 2 * NA_WIN_COLS - 1
    sel_r = np.zeros((3, NA_QROWS, NA_KROWS, n_dr), np.float32)
    in_r = np.zeros((3, NA_QROWS, NA_KROWS), np.float32)
    for p, r0 in enumerate((0, 2 * NA_QROWS, rows - NA_QROWS)):
        start = int(np.clip(r0 - NA_WIN_ROWS // 2, 0, rows - NA_KROWS))
        for lr in range(NA_QROWS):
            r = r0 + lr
            rs = int(np.clip(r - NA_WIN_ROWS // 2, 0, rows - NA_WIN_ROWS))
            for kl in range(NA_KROWS):
                kr = start + kl
                if rs <= kr < rs + NA_WIN_ROWS:
                    in_r[p, lr, kl] = 1.0
                    sel_r[p, lr, kl, kr - r + NA_WIN_ROWS - 1] = 1.0
    sel_c = np.zeros((GRID_W, GRID_W, n_dc), np.float32)
    in_c = np.zeros((GRID_W, GRID_W), np.float32)
    for c in range(GRID_W):
        cs = int(np.clip(c - NA_WIN_COLS // 2, 0, GRID_W - NA_WIN_COLS))
        for kc in range(cs, cs + NA_WIN_COLS):
            in_c[c, kc] = 1.0
            dc = int(np.clip(kc - c, -(NA_WIN_COLS - 1), NA_WIN_COLS - 1))
            sel_c[c, kc, dc + NA_WIN_COLS - 1] = 1.0
    hp = lax.Precision.HIGHEST
    rows_sel = jnp.einsum('plkd,hde->phlke', jnp.asarray(sel_r), rel_bias.astype(F32), precision=hp)
    vals = jnp.einsum('phlke,cje->phlckj', rows_sel, jnp.asarray(sel_c), precision=hp)
    inside = jnp.asarray(in_r[:, None, :, None, :, None] * in_c[None, None, None, :, None, :])
    tab = jnp.where(inside > 0.5, vals * LOG2E, NEG_INF)
    return tab.reshape(3, rel_bias.shape[0], NA_QROWS * GRID_W, NA_KROWS * GRID_W)


def _na_attention(q, k, v, rel_bias):
    b, s, _ = q.shape
    rows = s // GRID_W
    assert rows % NA_QROWS == 0 and rows >= NA_KROWS and rows // NA_QROWS >= 3
    assert NA_QROWS <= NA_WIN_ROWS // 2
    nq, nk = NA_QROWS * GRID_W, NA_KROWS * GRID_W
    bias = _na_bias_tables(rel_bias, rows)
    blk = pl.BlockSpec((1, s, LANES), lambda bi, hp: (bi, 0, hp))
    return pl.pallas_call(
        _na_kernel, grid=(b, A_HEADS // 2),
        in_specs=[blk, blk, blk, pl.BlockSpec((3, 2, nq, nk), lambda bi, hp: (0, hp, 0, 0))],
        out_specs=blk, out_shape=jax.ShapeDtypeStruct((b, s, A_W), BF16),
        compiler_params=_cparams(("parallel", "arbitrary")), name="na_attention")(q, k, v, bias)


def _swa_kernel(sink_ref, q_ref, k_ref, v_ref, rep_ref, o_ref, kt_ref, vt_ref):
    s_len = q_ref.shape[1]
    n_groups = B_HEADS // B_KV_HEADS
    gw = n_groups * HEAD_DIM
    lane = lax.broadcasted_iota(jnp.int32, (1, gw), 1)
    rel = (lax.broadcasted_iota(jnp.int32, (SWA_Q, SWA_K), 0)
           - lax.broadcasted_iota(jnp.int32, (SWA_Q, SWA_K), 1))

    for g in range(B_KV_HEADS):
        kt_ref[...] = jnp.dot(k_ref[0], rep_ref[g], preferred_element_type=F32).astype(BF16)
        vt_ref[...] = jnp.dot(v_ref[0], rep_ref[g], preferred_element_type=F32).astype(BF16)

        for i in range(s_len // SWA_Q):
            q0 = i * SWA_Q
            k0 = min(max(q0 - B_WINDOW, 0), s_len - SWA_K)
            q = q_ref[0, q0:q0 + SWA_Q, g * gw:(g + 1) * gw]
            kt = kt_ref[k0:k0 + SWA_K, :]
            vt = vt_ref[k0:k0 + SWA_K, :]
            valid = jnp.abs(rel + (q0 - k0)) <= B_WINDOW
            acc = jnp.zeros((SWA_Q, gw), F32)
            for j in range(n_groups):
                sel = (lane >= j * HEAD_DIM) & (lane < (j + 1) * HEAD_DIM)
                sink = sink_ref[g * n_groups + j]
                qm = jnp.where(sel, q, jnp.zeros_like(q))
                s = lax.dot_general(qm, kt, (((1,), (1,)), ((), ())), preferred_element_type=F32)
                s = jnp.where(valid, s, NEG_INF)
                m = jnp.maximum(jnp.max(s, axis=-1, keepdims=True), sink)
                e = jnp.exp2(s - m)
                den = jnp.sum(e, axis=-1, keepdims=True) + jnp.exp2(sink - m)
                o = jnp.dot(e.astype(BF16), vt, preferred_element_type=F32) / den
                acc = jnp.where(sel, o, acc)
            o_ref[0, q0:q0 + SWA_Q, g * gw:(g + 1) * gw] = acc.astype(BF16)


def _replication_matrices(n_kv):
    rep = np.zeros((n_kv, n_kv * HEAD_DIM, 4 * HEAD_DIM), np.float32)
    for g in range(n_kv):
        for j in range(4 * HEAD_DIM):
            rep[g, g * HEAD_DIM + j % HEAD_DIM, j] = 1.0
    return jnp.asarray(rep, BF16)


def _swa_attention(q, k, v, sink):
    b, s, _ = q.shape
    assert s % SWA_Q == 0 and s >= SWA_K and SWA_K >= SWA_Q + 2 * B_WINDOW
    rep = _replication_matrices(B_KV_HEADS)
    return pl.pallas_call(
        _swa_kernel, grid=(b,),
        in_specs=[pl.BlockSpec(memory_space=pltpu.SMEM),
                  pl.BlockSpec((1, s, B_QW), lambda bi: (bi, 0, 0)),
                  pl.BlockSpec((1, s, B_KVW), lambda bi: (bi, 0, 0)),
                  pl.BlockSpec((1, s, B_KVW), lambda bi: (bi, 0, 0)),
                  pl.BlockSpec((B_KV_HEADS, B_KVW, 4 * HEAD_DIM), lambda bi: (0, 0, 0))],
        out_specs=pl.BlockSpec((1, s, B_QW), lambda bi: (bi, 0, 0)),
        out_shape=jax.ShapeDtypeStruct((b, s, B_QW), BF16),
        scratch_shapes=[pltpu.VMEM((s, 4 * HEAD_DIM), BF16), pltpu.VMEM((s, 4 * HEAD_DIM), BF16)],
        compiler_params=_cparams(("parallel",)), name="swa_attention")(sink.astype(F32) * LOG2E, q, k, v, rep)


def _dense_kernel(q_ref, k_ref, v_ref, rep_ref, o_ref, kt_ref, vt_ref):
    s_len = q_ref.shape[1]
    gw = 4 * HEAD_DIM
    lane = lax.broadcasted_iota(jnp.int32, (1, gw), 1)
    kt_ref[...] = jnp.dot(k_ref[0], rep_ref[0], preferred_element_type=F32).astype(BF16)
    vt_ref[...] = jnp.dot(v_ref[0], rep_ref[0], preferred_element_type=F32).astype(BF16)

    def body(i, carry):
        for u in range(DENSE_UNROLL):
            q0 = pl.multiple_of((i * DENSE_UNROLL + u) * DENSE_Q, DENSE_Q)
            q = q_ref[0, pl.ds(q0, DENSE_Q), :]
            acc = jnp.zeros((DENSE_Q, gw), F32)
            for j in range(4):
                sel = (lane >= j * HEAD_DIM) & (lane < (j + 1) * HEAD_DIM)
                qm = jnp.where(sel, q, jnp.zeros_like(q))
                s = lax.dot_general(qm, kt_ref[...], (((1,), (1,)), ((), ())), preferred_element_type=F32)
                m = jnp.max(s, axis=-1, keepdims=True)
                p = jnp.exp2(s - m)
                l = jnp.sum(p, axis=-1, keepdims=True)
                o = jnp.dot(p.astype(BF16), vt_ref[...], preferred_element_type=F32) / l
                acc = jnp.where(sel, o, acc)
            o_ref[0, pl.ds(q0, DENSE_Q), :] = acc.astype(BF16)
        return carry

    lax.fori_loop(0, s_len // (DENSE_Q * DENSE_UNROLL), body, 0)


def _dense_attention(q, k, v):
    b, s, _ = q.shape
    gw = 4 * HEAD_DIM
    rep = _replication_matrices(C_KV_HEADS)
    return pl.pallas_call(
        _dense_kernel, grid=(b, C_KV_HEADS),
        in_specs=[pl.BlockSpec((1, s, gw), lambda bi, g: (bi, 0, g)),
                  pl.BlockSpec((1, s, C_KVW), lambda bi, g: (bi, 0, 0)),
                  pl.BlockSpec((1, s, C_KVW), lambda bi, g: (bi, 0, 0)),
                  pl.BlockSpec((1, C_KVW, gw), lambda bi, g: (g, 0, 0))],
        out_specs=pl.BlockSpec((1, s, gw), lambda bi, g: (bi, 0, g)),
        out_shape=jax.ShapeDtypeStruct((b, s, C_QW), BF16),
        scratch_shapes=[pltpu.VMEM((s, gw), BF16), pltpu.VMEM((s, gw), BF16)],
        compiler_params=_cparams(("parallel", "arbitrary")), name="dense_attention")(q, k, v, rep)


def _out_proj_kernel(*refs, n_parts):
    it = iter(refs)
    x_ref = next(it)
    parts = [next(it) for _ in range(n_parts)]
    w_ref = next(it)
    gain_ref = next(it)
    wr_ref = next(it)
    xo_ref = next(it)
    hpk_ref = next(it)
    aff_ref = next(it)
    afft_ref = next(it)
    h_ref = next(it)

    y = x_ref[...]
    c0 = 0
    for p_ref in parts:
        wp = p_ref.shape[1]
        y = y + jnp.dot(p_ref[...], w_ref[c0:c0 + wp, :], preferred_element_type=F32)
        c0 += wp
    xo_ref[...] = y
    h = _rms_rows(y) * gain_ref[...]
    h_ref[...] = h
    _to_packed(h_ref, hpk_ref, TM)

    h_hi = h.astype(BF16)
    h_lo = (h - h_hi.astype(F32)).astype(BF16)
    r_hi = jnp.dot(h_hi, wr_ref[...], preferred_element_type=F32)
    r_lo = jnp.dot(h_lo, wr_ref[:, :LANES], preferred_element_type=F32)
    lane = lax.broadcasted_iota(jnp.int32, (1, LANES), 1)
    logits = r_hi[:, :LANES] + r_hi[:, LANES:] + r_lo
    logits = jnp.where(lane < N_EXPERTS, logits, NEG_INF)
    e = jnp.exp(logits - jnp.max(logits, axis=-1, keepdims=True))
    aff = e / jnp.sum(e, axis=-1, keepdims=True)
    aff_ref[...] = aff
    afft_ref[0] = jnp.transpose(aff)[:N_EXPERTS, :]


def _out_proj(x2d, parts, w_out_bf16, ffn_gain, w_router, batch, seq):
    m = x2d.shape[0]
    tiles_per_seq = seq // TM
    wr = w_router.astype(F32)
    wr_hi = wr.astype(BF16)
    wr_lo = (wr - wr_hi.astype(F32)).astype(BF16)
    wr_split = jnp.zeros((D_MODEL, 2 * LANES), BF16)
    wr_split = wr_split.at[:, :N_EXPERTS].set(wr_hi).at[:, LANES:LANES + N_EXPERTS].set(wr_lo)
    in_specs = [pl.BlockSpec((TM, D_MODEL), lambda i: (i, 0))]
    in_specs += [pl.BlockSpec((TM, p.shape[1]), lambda i: (i, 0)) for p in parts]
    in_specs += [pl.BlockSpec((D_MODEL, D_MODEL), lambda i: (0, 0)),
                 pl.BlockSpec((1, D_MODEL), lambda i: (0, 0)),
                 pl.BlockSpec((D_MODEL, 2 * LANES), lambda i: (0, 0))]
    out_shape = [jax.ShapeDtypeStruct((m, D_MODEL), F32),
                 jax.ShapeDtypeStruct((m * PACK_TILES, LANES), U32),
                 jax.ShapeDtypeStruct((m, LANES), F32),
                 jax.ShapeDtypeStruct((batch, N_EXPERTS, seq), F32)]
    out_specs = [pl.BlockSpec((TM, D_MODEL), lambda i: (i, 0)),
                 pl.BlockSpec((TM * PACK_TILES, LANES), lambda i: (i, 0)),
                 pl.BlockSpec((TM, LANES), lambda i: (i, 0)),
                 pl.BlockSpec((1, N_EXPERTS, TM), lambda i: (i // tiles_per_seq, 0, i % tiles_per_seq))]
    kern = functools.partial(_out_proj_kernel, n_parts=len(parts))
    return pl.pallas_call(
        kern, grid=(m // TM,), in_specs=in_specs, out_specs=out_specs, out_shape=out_shape,
        scratch_shapes=[pltpu.VMEM((TM, D_MODEL), F32)],
        compiler_params=_cparams(("parallel",)), name="out_proj_router")(
            x2d, *parts, w_out_bf16, ffn_gain.reshape(1, D_MODEL), wr_split)


def _topk_kernel(afft_ref, aff_ref, ids_ref, gate_ref, res_ref, *, cap):
    s_len = afft_ref.shape[2]
    at = afft_ref[0]
    bits = pltpu.bitcast(at, jnp.int32)

    def count(mask):
        return jnp.sum(jnp.where(mask, 1.0, 0.0), axis=1, keepdims=True)

    def bit_step(k, thr):
        cand = thr | jnp.left_shift(jnp.int32(1), 30 - k)
        return jnp.where(count(bits >= cand) >= cap, cand, thr)

    thr = lax.fori_loop(0, 31, bit_step, jnp.zeros((N_EXPERTS, 1), jnp.int32))
    gt = bits > thr
    eq = bits == thr
    need = cap - count(gt)

    ri = lax.broadcasted_iota(jnp.int32, (LANES, LANES), 0)
    ci = lax.broadcasted_iota(jnp.int32, (LANES, LANES), 1)
    upper = jnp.where(ri < ci, 1.0, 0.0).astype(BF16)

    def prefix(mask_f32):
        out = []
        run = jnp.zeros((N_EXPERTS, 1), F32)
        for c in range(s_len // LANES):
            blk = mask_f32[:, c * LANES:(c + 1) * LANES]
            out.append(jnp.dot(blk.astype(BF16), upper, preferred_element_type=F32) + run)
            run = run + jnp.sum(blk, axis=1, keepdims=True)
        return out

    eq_f = jnp.where(eq, 1.0, 0.0)
    eq_rank = prefix(eq_f)
    sel_tiles = []
    for c in range(s_len // LANES):
        sl = slice(c * LANES, (c + 1) * LANES)
        sel_tiles.append(jnp.where(gt[:, sl] | (eq[:, sl] & (eq_rank[c] < need)), 1.0, 0.0))
    sel_f = jnp.concatenate(sel_tiles, axis=1)
    pos = jnp.concatenate(prefix(sel_f), axis=1)
    key = jnp.where(sel_f > 0.5, pos, -1.0)

    a = aff_ref[0]
    a_hi = a.astype(BF16).astype(F32)
    r1 = a - a_hi
    a_mid = r1.astype(BF16).astype(F32)
    a_lo = r1 - a_mid
    lane = lax.broadcasted_iota(jnp.int32, (s_len, LANES), 1)
    tok = lax.broadcasted_iota(jnp.int32, (s_len, LANES), 0)
    packed = a_hi + pltpu.roll(a_mid, N_EXPERTS, 1) + pltpu.roll(a_lo, 2 * N_EXPERTS, 1)
    packed = jnp.where(lane == 64, (tok // 64).astype(F32), packed)
    packed = jnp.where(lane == 65, (tok % 64).astype(F32), packed)
    rmat = packed.astype(BF16)

    slot = lax.broadcasted_iota(jnp.int32, (cap, 1), 0).astype(F32)
    for e in range(N_EXPERTS):
        onehot_t = jnp.where(key[e:e + 1, :] == slot, 1.0, 0.0).astype(BF16)
        res_ref[...] = jnp.dot(onehot_t, rmat, preferred_element_type=F32)
        res = res_ref[...]
        res_t = jnp.transpose(res)
        ids_ref[0, e:e + 1, :] = (res_t[64:65, :] * 64.0 + res_t[65:66, :]).astype(jnp.int32)
        gate = (res[:, e:e + 1] + res[:, N_EXPERTS + e:N_EXPERTS + e + 1]
                + res[:, 2 * N_EXPERTS + e:2 * N_EXPERTS + e + 1])
        gate_ref[0, e] = jnp.broadcast_to(gate, (cap, LANES))


def _topk(aff_t, aff, cap):
    b, _, s = aff_t.shape
    kern = functools.partial(_topk_kernel, cap=cap)
    return pl.pallas_call(
        kern, grid=(b,),
        in_specs=[pl.BlockSpec((1, N_EXPERTS, s), lambda bi: (bi, 0, 0)),
                  pl.BlockSpec((1, s, LANES), lambda bi: (bi, 0, 0))],
        out_specs=[pl.BlockSpec((1, N_EXPERTS, cap), lambda bi: (bi, 0, 0)),
                   pl.BlockSpec((1, N_EXPERTS, cap, LANES), lambda bi: (bi, 0, 0, 0))],
        out_shape=[jax.ShapeDtypeStruct((b, N_EXPERTS, cap), jnp.int32),
                   jax.ShapeDtypeStruct((b, N_EXPERTS, cap, LANES), F32)],
        scratch_shapes=[pltpu.VMEM((cap, LANES), F32)],
        compiler_params=_cparams(("parallel",)), name="expert_topk")(aff_t, aff)


MOE_GROUP = 16
SCATTER_GROUP = 8
MERGE_ROWS = 1024


MOE_EXPERTS_PER_STEP = 4
FFN_ROWS = 512
FFN_F_CHUNK = 512


def _gather_kernel(ids_ref, h_ref, o_ref, *, cap):
    for el in range(MOE_EXPERTS_PER_STEP):
        def body(jg, carry, el=el):
            for u in range(MOE_GROUP):
                j = jg * MOE_GROUP + u
                o_ref[el, j] = h_ref[0, ids_ref[0, 0, el * cap + j]]
            return carry
        lax.fori_loop(0, cap // MOE_GROUP, body, 0)


def _ffn_kernel(xe_ref, gate_ref, wg_ref, wu_ref, wd_ref, o_ref, xb_ref):
    pair = 2 * SUBLANES
    for i in range(FFN_ROWS // pair):
        for c in range(PACK_TILES):
            base = i * pair * PACK_TILES + c
            lo_a, hi_a = _unpack_pair(xe_ref[0, pl.ds(base, SUBLANES, stride=PACK_TILES), :])
            lo_b, hi_b = _unpack_pair(
                xe_ref[0, pl.ds(base + SUBLANES * PACK_TILES, SUBLANES, stride=PACK_TILES), :])
            rows = slice(i * pair, (i + 1) * pair)
            xb_ref[rows, c * LANES:(c + 1) * LANES] = jnp.concatenate([lo_a, lo_b], axis=0).astype(BF16)
            xb_ref[rows, HALF_D + c * LANES:HALF_D + (c + 1) * LANES] = (
                jnp.concatenate([hi_a, hi_b], axis=0).astype(BF16))
    xe = xb_ref[...]
    y = None
    for f0 in range(0, D_FF, FFN_F_CHUNK):
        hg = jnp.dot(xe, wg_ref[0, 0, :, f0:f0 + FFN_F_CHUNK], preferred_element_type=F32)
        hu = jnp.dot(xe, wu_ref[0, 0, :, f0:f0 + FFN_F_CHUNK], preferred_element_type=F32)
        hid = (hg * (1.0 / (1.0 + jnp.exp(-hg))) * hu).astype(BF16)
        part = jnp.dot(hid, wd_ref[0, 0, f0:f0 + FFN_F_CHUNK, :], preferred_element_type=F32)
        y = part if y is None else y + part
    cap = gate_ref.shape[2]
    for i in range(FFN_ROWS // SUBLANES):
        r0 = i * SUBLANES
        gate = gate_ref[r0 // cap, 0, r0 % cap:r0 % cap + SUBLANES, :]
        for c in range(ROW_TILES):
            o_ref[0, pl.ds(r0 * ROW_TILES + c, SUBLANES, stride=ROW_TILES), :] = (
                y[r0:r0 + SUBLANES, c * LANES:(c + 1) * LANES] * gate)


def _scatter_kernel(ids_ref, y_ref, x_ref, o_ref, acc_ref, *, cap, n_eg):
    step = pl.program_id(1)

    @pl.when(step == 0)
    def _():
        acc_ref[...] = jnp.zeros_like(acc_ref)

    @pl.when(step < n_eg)
    def _():
        for el in range(MOE_EXPERTS_PER_STEP):
            def body(jg, carry, el=el):
                rows = []
                for u in range(SCATTER_GROUP):
                    j = jg * SCATTER_GROUP + u
                    rows.append(pl.multiple_of(ids_ref[0, 0, el * cap + j] * ROW_TILES, ROW_TILES))
                vals = []
                for u in range(SCATTER_GROUP):
                    j = jg * SCATTER_GROUP + u
                    src = pl.multiple_of(j * ROW_TILES, ROW_TILES)
                    vals.append(acc_ref[pl.ds(rows[u], ROW_TILES), :] + y_ref[el, pl.ds(src, ROW_TILES), :])
                for u in range(SCATTER_GROUP):
                    acc_ref[pl.ds(rows[u], ROW_TILES), :] = vals[u]
                return carry
            lax.fori_loop(0, cap // SCATTER_GROUP, body, 0)

    @pl.when(step >= n_eg)
    def _():
        first = (step - n_eg) * (MERGE_ROWS * ROW_TILES)

        def body(i, carry):
            r0 = pl.multiple_of(i * SUBLANES, SUBLANES)
            base = first + i * (SUBLANES * ROW_TILES)
            for c in range(ROW_TILES):
                cols = slice(c * LANES, (c + 1) * LANES)
                o_ref[pl.ds(r0, SUBLANES), cols] = (
                    x_ref[pl.ds(r0, SUBLANES), cols]
                    + acc_ref[pl.ds(base + c, SUBLANES, stride=ROW_TILES), :])
            return carry
        lax.fori_loop(0, MERGE_ROWS // SUBLANES, body, 0)


def _moe(x2d, ids, gates, h_pk, wg, wu, wd, layer, cap):
    b = ids.shape[0]
    seq = h_pk.shape[0] // (b * PACK_TILES)
    n_eg = N_EXPERTS // MOE_EXPERTS_PER_STEP
    step_ids = MOE_EXPERTS_PER_STEP * cap
    ids_g = ids.reshape(b * n_eg, 1, step_ids)
    smem_spec = pl.BlockSpec((1, 1, step_ids), lambda bi, eg: (bi * n_eg + eg, 0, 0), memory_space=pltpu.SMEM)

    xe_pk = pl.pallas_call(
        functools.partial(_gather_kernel, cap=cap), grid=(b, n_eg),
        in_specs=[smem_spec, pl.BlockSpec((1, seq, PACK_TILES, LANES), lambda bi, eg: (bi, 0, 0, 0))],
        out_specs=pl.BlockSpec((MOE_EXPERTS_PER_STEP, cap, PACK_TILES, LANES), lambda bi, eg: (eg, bi, 0, 0)),
        out_shape=jax.ShapeDtypeStruct((N_EXPERTS, b * cap, PACK_TILES, LANES), U32),
        compiler_params=_cparams(("parallel", "arbitrary")), name="moe_gather")(
            ids_g, h_pk.reshape(b, seq, PACK_TILES, LANES))

    assert (b * cap) % FFN_ROWS == 0 and FFN_ROWS % cap == 0
    slot_rows = cap * ROW_TILES
    seqs_per_step = FFN_ROWS // cap
    ye_rt = pl.pallas_call(
        _ffn_kernel, grid=(N_EXPERTS, b * cap // FFN_ROWS),
        in_specs=[pl.BlockSpec((1, FFN_ROWS * PACK_TILES, LANES), lambda e, i: (e, i, 0)),
                  pl.BlockSpec((seqs_per_step, 1, cap, LANES), lambda e, i: (i, e, 0, 0)),
                  pl.BlockSpec((1, 1, D_MODEL, D_FF), lambda e, i: (layer, e, 0, 0)),
                  pl.BlockSpec((1, 1, D_MODEL, D_FF), lambda e, i: (layer, e, 0, 0)),
                  pl.BlockSpec((1, 1, D_FF, D_MODEL), lambda e, i: (layer, e, 0, 0))],
        out_specs=pl.BlockSpec((1, FFN_ROWS * ROW_TILES, LANES), lambda e, i: (e, i, 0)),
        out_shape=jax.ShapeDtypeStruct((N_EXPERTS, b * slot_rows, LANES), F32),
        scratch_shapes=[pltpu.VMEM((FFN_ROWS, D_MODEL), BF16)],
        compiler_params=_cparams(("parallel", "arbitrary")), name="moe_ffn")(
            xe_pk.reshape(N_EXPERTS, b * cap * PACK_TILES, LANES), gates, wg, wu, wd)

    assert seq % MERGE_ROWS == 0
    n_merge = seq // MERGE_ROWS

    def eg_of(st):
        return jnp.minimum(st, n_eg - 1)

    def rows_of(bi, st):
        return (bi * n_merge + jnp.maximum(st - n_eg, 0), 0)

    smem_step = pl.BlockSpec((1, 1, step_ids), lambda bi, st: (bi * n_eg + eg_of(st), 0, 0),
                             memory_space=pltpu.SMEM)
    return pl.pallas_call(
        functools.partial(_scatter_kernel, cap=cap, n_eg=n_eg), grid=(b, n_eg + n_merge),
        in_specs=[smem_step,
                  pl.BlockSpec((MOE_EXPERTS_PER_STEP, slot_rows, LANES), lambda bi, st: (eg_of(st), bi, 0)),
                  pl.BlockSpec((MERGE_ROWS, D_MODEL), rows_of)],
        out_specs=pl.BlockSpec((MERGE_ROWS, D_MODEL), rows_of),
        out_shape=jax.ShapeDtypeStruct((b * seq, D_MODEL), F32),
        scratch_shapes=[pltpu.VMEM((seq * ROW_TILES, LANES), F32)],
        compiler_params=_cparams(("parallel", "arbitrary")), name="moe_scatter_merge")(
            ids_g, ye_rt, x2d)


def _rope_tables(seq):
    t = jnp.arange(seq).astype(F32)
    inv = jnp.power(jnp.float32(ROPE_THETA), -jnp.arange(0, HEAD_DIM, 2, dtype=F32) / HEAD_DIM)
    ang = t[:, None] * inv[None, :]
    cos = jnp.concatenate([jnp.cos(ang), jnp.cos(ang)], axis=1)
    sin = jnp.concatenate([-jnp.sin(ang), jnp.sin(ang)], axis=1)
    return jnp.tile(cos, (1, 4)), jnp.tile(sin, (1, 4))


def _axial_tables(seq):
    t = jnp.arange(seq)
    half = HEAD_DIM // 2
    inv = jnp.power(jnp.float32(ROPE_THETA), -jnp.arange(0, half, 2, dtype=F32) / half)
    ang_r = (t // GRID_W).astype(F32)[:, None] * inv[None, :]
    ang_c = (t % GRID_W).astype(F32)[:, None] * inv[None, :]
    cos = jnp.concatenate([jnp.cos(ang_r), jnp.cos(ang_r), jnp.cos(ang_c), jnp.cos(ang_c)], axis=1)
    sin = jnp.concatenate([-jnp.sin(ang_r), jnp.sin(ang_r), -jnp.sin(ang_c), jnp.sin(ang_c)], axis=1)
    return jnp.tile(cos, (1, 4)), jnp.tile(sin, (1, 4))


def _col_gain(pieces):
    cols = []
    for gain, n_heads, scale in pieces:
        if gain is None:
            cols.append(jnp.ones((n_heads * HEAD_DIM,), F32))
        else:
            cols.append(jnp.tile(gain.astype(F32) * scale, n_heads))
    return jnp.concatenate(cols)[None, :]


def _chunks(start, width, kind, oi):
    out = []
    step = 256 if width % 256 == 0 else LANES
    for c in range(0, width, step):
        out.append((start + c, step, kind, oi, c))
    return out


def _moe_block(x2d, parts, w_out, ffn_gain, w_router, wg, wu, wd, layer, batch, seq):
    cap = EC_CAPACITY_FACTOR * seq // N_EXPERTS
    x_new, h_pk, aff, aff_t = _out_proj(x2d, parts, w_out.astype(BF16), ffn_gain, w_router, batch, seq)
    ids, gates = _topk(aff_t, aff.reshape(batch, seq, LANES), cap)
    return _moe(x_new, ids, gates, h_pk, wg, wu, wd, layer, cap)


def kernel(x, attn_norm_even, w_in_even, q_norm_a, k_norm_a, rel_bias_a, q_norm_b, k_norm_b, sink_b,
           w_out_even, attn_norm_odd, w_in_odd, q_norm_c, k_norm_c, w_out_odd,
           ffn_norm, w_router, w_gate, w_up, w_down):
    batch, seq, _ = x.shape
    m = batch * seq
    scale = HEAD_DIM ** -0.5 * LOG2E
    x2d = x.reshape(m, D_MODEL)
    depth = ffn_norm.shape[0]
    wg_all, wu_all, wd_all = w_gate.astype(BF16), w_up.astype(BF16), w_down.astype(BF16)
    for layer in range(depth):
        i = layer // 2
        if layer % 2 == 0:
            cos_t, sin_t = _rope_tables(seq)
            colgain = _col_gain([(q_norm_a[i], A_HEADS, scale), (k_norm_a[i], A_HEADS, 1.0),
                                 (None, A_HEADS, 1.0), (q_norm_b[i], B_HEADS, scale),
                                 (k_norm_b[i], B_KV_HEADS, 1.0), (None, B_KV_HEADS, 1.0)])
            sections = (_chunks(0, A_W, "norm", 0) + _chunks(A_W, A_W, "norm", 1)
                        + _chunks(2 * A_W, A_W, "plain", 2) + _chunks(3 * A_W, B_QW, "rope", 3)
                        + _chunks(3 * A_W + B_QW, B_KVW, "rope", 4)
                        + _chunks(3 * A_W + B_QW + B_KVW, B_KVW, "plain", 5))
            widths = [A_W, A_W, A_W, B_QW, B_KVW, B_KVW]
            outs = _norm_proj(x2d, attn_norm_even[i], w_in_even[i].astype(BF16), colgain,
                              cos_t, sin_t, sections, widths, HEAD_DIM // 2, seq)
            qa, ka, va, qb, kb, vb = [o.reshape(batch, seq, -1) for o in outs]
            out_a = _na_attention(qa, ka, va, rel_bias_a[i])
            out_b = _swa_attention(qb, kb, vb, sink_b[i])
            parts = [out_a.reshape(m, A_W), out_b.reshape(m, B_QW)]
            w_out = w_out_even[i]
        else:
            cos_t, sin_t = _axial_tables(seq)
            colgain = _col_gain([(q_norm_c[i], C_HEADS, scale), (k_norm_c[i], C_KV_HEADS, 1.0),
                                 (None, C_KV_HEADS, 1.0)])
            sections = (_chunks(0, C_QW, "rope", 0) + _chunks(C_QW, C_KVW, "rope", 1)
                        + _chunks(C_QW + C_KVW, C_KVW, "plain", 2))
            widths = [C_QW, C_KVW, C_KVW]
            outs = _norm_proj(x2d, attn_norm_odd[i], w_in_odd[i].astype(BF16), colgain,
                              cos_t, sin_t, sections, widths, HEAD_DIM // 4, seq)
            qc, kc, vc = [o.reshape(batch, seq, -1) for o in outs]
            parts = [_dense_attention(qc, kc, vc).reshape(m, C_QW)]
            w_out = w_out_odd[i]
        x2d = _moe_block(x2d, parts, w_out, ffn_norm[layer], w_router[layer],
                         wg_all, wu_all, wd_all, layer, batch, seq)
    return x2d.reshape(batch, seq, D_MODEL)
```

```python
import functools

import numpy as np
import jax
import jax.numpy as jnp
from jax import lax
from jax.experimental import pallas as pl
from jax.experimental.pallas import tpu as pltpu

D_MODEL = 1024
HEAD_DIM = 64
GRID_W = 64
ROPE_THETA = 10000.0
EPS = 1e-6
NEG_INF = -1e30
LOG2E = 1.4426950408889634

A_HEADS = 8
NA_WIN_ROWS = 8
NA_WIN_COLS = 16
B_HEADS = 8
B_KV_HEADS = 2
B_WINDOW = 128
C_HEADS = 16
C_KV_HEADS = 4
N_EXPERTS = 16
EC_CAPACITY_FACTOR = 2
D_FF = 2048

A_W = A_HEADS * HEAD_DIM
B_QW = B_HEADS * HEAD_DIM
B_KVW = B_KV_HEADS * HEAD_DIM
C_QW = C_HEADS * HEAD_DIM
C_KVW = C_KV_HEADS * HEAD_DIM

SUBLANES = 8
LANES = 128
ROW_TILES = D_MODEL // LANES
VMEM_LIMIT = 56 * 1024 * 1024

TM = 512
NA_QROWS = 4
NA_KROWS = 12
SWA_Q = 256
SWA_K = 512
DENSE_Q = 256
DENSE_UNROLL = 8

F32 = jnp.float32
BF16 = jnp.bfloat16


def _cparams(sem):
    return pltpu.CompilerParams(dimension_semantics=sem, vmem_limit_bytes=VMEM_LIMIT)


def _rms_rows(x):
    return x * lax.rsqrt(jnp.mean(x * x, axis=-1, keepdims=True) + EPS)


PACK_TILES = ROW_TILES // 2
HALF_D = D_MODEL // 2
U32 = jnp.uint32


def _pack_pair(lo, hi):
    lo_bits = pltpu.bitcast(lo.astype(BF16).astype(F32), U32) >> 16
    hi_bits = pltpu.bitcast(hi.astype(BF16).astype(F32), U32) & jnp.uint32(0xFFFF0000)
    return lo_bits | hi_bits


def _unpack_pair(words):
    lo = pltpu.bitcast(words << 16, F32)
    hi = pltpu.bitcast(words & jnp.uint32(0xFFFF0000), F32)
    return lo, hi


def _to_packed(src_ref, dst_ref, rows):
    def body(i, carry):
        r0 = pl.multiple_of(i * SUBLANES, SUBLANES)
        base = i * (SUBLANES * PACK_TILES)
        for c in range(PACK_TILES):
            lo = src_ref[pl.ds(r0, SUBLANES), c * LANES:(c + 1) * LANES]
            hi = src_ref[pl.ds(r0, SUBLANES), HALF_D + c * LANES:HALF_D + (c + 1) * LANES]
            dst_ref[pl.ds(base + c, SUBLANES, stride=PACK_TILES), :] = _pack_pair(lo, hi)
        return carry
    lax.fori_loop(0, rows // SUBLANES, body, 0)


def _norm_proj_kernel(x_ref, gain_ref, w_ref, cg_ref, cos_ref, sin_ref, bd_ref, *outs, sections, half):
    hb = (_rms_rows(x_ref[...]) * gain_ref[...]).astype(BF16)

    ys = [jnp.dot(hb, w_ref[:, c0:c0 + width], preferred_element_type=F32)
          for (c0, width, _, _, _) in sections]
    sss = []
    for y, (c0, width, kind, _, _) in zip(ys, sections):
        if kind == "plain":
            sss.append(None)
            continue
        y2 = y * y
        hi = y2.astype(BF16)
        lo = (y2 - hi.astype(F32)).astype(BF16)
        bd = bd_ref[:width, :width]
        sss.append(jnp.dot(hi, bd, preferred_element_type=F32)
                   + jnp.dot(lo, bd, preferred_element_type=F32))
    for y, ss, (c0, width, kind, oi, oc) in zip(ys, sss, sections):
        if kind != "plain":
            y = y * lax.rsqrt(ss + EPS) * cg_ref[:, c0:c0 + width]
        if kind == "rope":
            lane = lax.broadcasted_iota(jnp.int32, (1, width), 1)
            first = (lane % (2 * half)) < half
            part = jnp.where(first, pltpu.roll(y, width - half, 1), pltpu.roll(y, half, 1))
            y = y * cos_ref[:, :width] + part * sin_ref[:, :width]
        outs[oi][:, oc:oc + width] = y.astype(BF16)


def _norm_proj(x2d, gain, w_bf16, colgain, cos_t, sin_t, sections, out_widths, half, seq):
    m = x2d.shape[0]
    n_in = w_bf16.shape[1]
    bd = jnp.asarray(np.kron(np.eye(4), np.full((HEAD_DIM, HEAD_DIM), 1.0 / HEAD_DIM)), BF16)
    tiles_per_seq = seq // TM
    in_specs = [
        pl.BlockSpec((TM, D_MODEL), lambda i: (i, 0)),
        pl.BlockSpec((1, D_MODEL), lambda i: (0, 0)),
        pl.BlockSpec((D_MODEL, n_in), lambda i: (0, 0)),
        pl.BlockSpec((1, n_in), lambda i: (0, 0)),
        pl.BlockSpec((TM, 256), lambda i: (i % tiles_per_seq, 0)),
        pl.BlockSpec((TM, 256), lambda i: (i % tiles_per_seq, 0)),
        pl.BlockSpec((256, 256), lambda i: (0, 0)),
    ]
    out_shape = [jax.ShapeDtypeStruct((m, wd), BF16) for wd in out_widths]
    out_specs = [pl.BlockSpec((TM, wd), lambda i: (i, 0)) for wd in out_widths]
    kern = functools.partial(_norm_proj_kernel, sections=tuple(sections), half=half)
    return pl.pallas_call(
        kern, grid=(m // TM,), in_specs=in_specs, out_specs=out_specs, out_shape=out_shape,
        compiler_params=_cparams(("parallel",)), name="norm_proj")(
            x2d, gain.reshape(1, D_MODEL), w_bf16, colgain, cos_t, sin_t, bd)


def _na_kernel(q_ref, k_ref, v_ref, bias_ref, o_ref):
    nq = NA_QROWS * GRID_W
    nk = NA_KROWS * GRID_W
    rows = q_ref.shape[1] // GRID_W
    n_blocks = rows // NA_QROWS
    lane = lax.broadcasted_iota(jnp.int32, (1, LANES), 1)
    head0 = lane < HEAD_DIM

    edge_keys = NA_WIN_ROWS * GRID_W
    for qb in range(n_blocks):
        r0 = qb * NA_QROWS
        start = min(max(r0 - NA_WIN_ROWS // 2, 0), rows - NA_KROWS)
        pat = 0 if qb == 0 else (2 if qb == n_blocks - 1 else 1)
        c0, c1 = {0: (0, edge_keys), 1: (0, nk), 2: (nk - edge_keys, nk)}[pat]
        qs = r0 * GRID_W
        ks = start * GRID_W + c0
        q = q_ref[0, qs:qs + nq, :]
        k = k_ref[0, ks:ks + (c1 - c0), :]
        v = v_ref[0, ks:ks + (c1 - c0), :]
        acc = None
        for hh in range(2):
            sel = head0 if hh == 0 else jnp.logical_not(head0)
            qm = jnp.where(sel, q, jnp.zeros_like(q))
            s = lax.dot_general(qm, k, (((1,), (1,)), ((), ())), preferred_element_type=F32)
            s = s + bias_ref[pat, hh, :, c0:c1]
            m = jnp.max(s, axis=-1, keepdims=True)
            p = jnp.exp2(s - m)
            l = jnp.sum(p, axis=-1, keepdims=True)
            o = jnp.dot(p.astype(BF16), v, preferred_element_type=F32) / l
            acc = o if acc is None else jnp.where(sel, o, acc)
        o_ref[0, qs:qs + nq, :] = acc.astype(BF16)


def _na_bias_tables(rel_bias, rows):
    n_dr, n_dc = 2 * NA_WIN_ROWS - 1, 2 * NA_WIN_COLS - 1
    sel_r = np.zeros((3, NA_QROWS, NA_KROWS, n_dr), np.float32)
    in_r = np.zeros((3, NA_QROWS, NA_KROWS), np.float32)
    for p, r0 in enumerate((0, 2 * NA_QROWS, rows - NA_QROWS)):
        start = int(np.clip(r0 - NA_WIN_ROWS // 2, 0, rows - NA_KROWS))
        for lr in range(NA_QROWS):
            r = r0 + lr
            rs = int(np.clip(r - NA_WIN_ROWS // 2, 0, rows - NA_WIN_ROWS))
            for kl in range(NA_KROWS):
                kr = start + kl
                if rs <= kr < rs + NA_WIN_ROWS:
                    in_r[p, lr, kl] = 1.0
                    sel_r[p, lr, kl, kr - r + NA_WIN_ROWS - 1] = 1.0
    sel_c = np.zeros((GRID_W, GRID_W, n_dc), np.float32)
    in_c = np.zeros((GRID_W, GRID_W), np.float32)
    for c in range(GRID_W):
        cs = int(np.clip(c - NA_WIN_COLS // 2, 0, GRID_W - NA_WIN_COLS))
        for kc in range(cs, cs + NA_WIN_COLS):
            in_c[c, kc] = 1.0
            dc = int(np.clip(kc - c, -(NA_WIN_COLS - 1), NA_WIN_COLS - 1))
            sel_c[c, kc, dc + NA_WIN_COLS - 1] = 1.0
    hp = lax.Precision.HIGHEST
    rows_sel = jnp.einsum('plkd,hde->phlke', jnp.asarray(sel_r), rel_bias.astype(F32), precision=hp)
    vals = jnp.einsum('phlke,cje->phlckj', rows_sel, jnp.asarray(sel_c), precision=hp)
    inside = jnp.asarray(in_r[:, None, :, None, :, None] * in_c[None, None, None, :, None, :])
    tab = jnp.where(inside > 0.5, vals * LOG2E, NEG_INF)
    return tab.reshape(3, rel_bias.shape[0], NA_QROWS * GRID_W, NA_KROWS * GRID_W)


def _na_attention(q, k, v, rel_bias):
    b, s, _ = q.shape
    rows = s // GRID_W
    assert rows % NA_QROWS == 0 and rows >= NA_KROWS and rows // NA_QROWS >= 3
    assert NA_QROWS <= NA_WIN_ROWS // 2
    nq, nk = NA_QROWS * GRID_W, NA_KROWS * GRID_W
    bias = _na_bias_tables(rel_bias, rows)
    blk = pl.BlockSpec((1, s, LANES), lambda bi, hp: (bi, 0, hp))
    return pl.pallas_call(
        _na_kernel, grid=(b, A_HEADS // 2),
        in_specs=[blk, blk, blk, pl.BlockSpec((3, 2, nq, nk), lambda bi, hp: (0, hp, 0, 0))],
        out_specs=blk, out_shape=jax.ShapeDtypeStruct((b, s, A_W), BF16),
        compiler_params=_cparams(("parallel", "arbitrary")), name="na_attention")(q, k, v, bias)


def _swa_kernel(sink_ref, q_ref, k_ref, v_ref, rep_ref, o_ref, kt_ref, vt_ref):
    s_len = q_ref.shape[1]
    n_groups = B_HEADS // B_KV_HEADS
    gw = n_groups * HEAD_DIM
    lane = lax.broadcasted_iota(jnp.int32, (1, gw), 1)
    rel = (lax.broadcasted_iota(jnp.int32, (SWA_Q, SWA_K), 0)
           - lax.broadcasted_iota(jnp.int32, (SWA_Q, SWA_K), 1))

    for g in range(B_KV_HEADS):
        kt_ref[...] = jnp.dot(k_ref[0], rep_ref[g], preferred_element_type=F32).astype(BF16)
        vt_ref[...] = jnp.dot(v_ref[0], rep_ref[g], preferred_element_type=F32).astype(BF16)

        for i in range(s_len // SWA_Q):
            q0 = i * SWA_Q
            k0 = min(max(q0 - B_WINDOW, 0), s_len - SWA_K)
            q = q_ref[0, q0:q0 + SWA_Q, g * gw:(g + 1) * gw]
            kt = kt_ref[k0:k0 + SWA_K, :]
            vt = vt_ref[k0:k0 + SWA_K, :]
            valid = jnp.abs(rel + (q0 - k0)) <= B_WINDOW
            acc = jnp.zeros((SWA_Q, gw), F32)
            for j in range(n_groups):
                sel = (lane >= j * HEAD_DIM) & (lane < (j + 1) * HEAD_DIM)
                sink = sink_ref[g * n_groups + j]
                qm = jnp.where(sel, q, jnp.zeros_like(q))
                s = lax.dot_general(qm, kt, (((1,), (1,)), ((), ())), preferred_element_type=F32)
                s = jnp.where(valid, s, NEG_INF)
                m = jnp.maximum(jnp.max(s, axis=-1, keepdims=True), sink)
                e = jnp.exp2(s - m)
                den = jnp.sum(e, axis=-1, keepdims=True) + jnp.exp2(sink - m)
                o = jnp.dot(e.astype(BF16), vt, preferred_element_type=F32) / den
                acc = jnp.where(sel, o, acc)
            o_ref[0, q0:q0 + SWA_Q, g * gw:(g + 1) * gw] = acc.astype(BF16)


def _replication_matrices(n_kv):
    rep = np.zeros((n_kv, n_kv * HEAD_DIM, 4 * HEAD_DIM), np.float32)
    for g in range(n_kv):
        for j in range(4 * HEAD_DIM):
            rep[g, g * HEAD_DIM + j % HEAD_DIM, j] = 1.0
    return jnp.asarray(rep, BF16)


def _swa_attention(q, k, v, sink):
    b, s, _ = q.shape
    assert s % SWA_Q == 0 and s >= SWA_K and SWA_K >= SWA_Q + 2 * B_WINDOW
    rep = _replication_matrices(B_KV_HEADS)
    return pl.pallas_call(
        _swa_kernel, grid=(b,),
        in_specs=[pl.BlockSpec(memory_space=pltpu.SMEM),
                  pl.BlockSpec((1, s, B_QW), lambda bi: (bi, 0, 0)),
                  pl.BlockSpec((1, s, B_KVW), lambda bi: (bi, 0, 0)),
                  pl.BlockSpec((1, s, B_KVW), lambda bi: (bi, 0, 0)),
                  pl.BlockSpec((B_KV_HEADS, B_KVW, 4 * HEAD_DIM), lambda bi: (0, 0, 0))],
        out_specs=pl.BlockSpec((1, s, B_QW), lambda bi: (bi, 0, 0)),
        out_shape=jax.ShapeDtypeStruct((b, s, B_QW), BF16),
        scratch_shapes=[pltpu.VMEM((s, 4 * HEAD_DIM), BF16), pltpu.VMEM((s, 4 * HEAD_DIM), BF16)],
        compiler_params=_cparams(("parallel",)), name="swa_attention")(sink.astype(F32) * LOG2E, q, k, v, rep)


def _dense_kernel(q_ref, k_ref, v_ref, rep_ref, o_ref, kt_ref, vt_ref):
    s_len = q_ref.shape[1]
    gw = 4 * HEAD_DIM
    lane = lax.broadcasted_iota(jnp.int32, (1, gw), 1)
    kt_ref[...] = jnp.dot(k_ref[0], rep_ref[0], preferred_element_type=F32).astype(BF16)
    vt_ref[...] = jnp.dot(v_ref[0], rep_ref[0], preferred_element_type=F32).astype(BF16)

    def body(i, carry):
        for u in range(DENSE_UNROLL):
            q0 = pl.multiple_of((i * DENSE_UNROLL + u) * DENSE_Q, DENSE_Q)
            q = q_ref[0, pl.ds(q0, DENSE_Q), :]
            acc = jnp.zeros((DENSE_Q, gw), F32)
            for j in range(4):
                sel = (lane >= j * HEAD_DIM) & (lane < (j + 1) * HEAD_DIM)
                qm = jnp.where(sel, q, jnp.zeros_like(q))
                s = lax.dot_general(qm, kt_ref[...], (((1,), (1,)), ((), ())), preferred_element_type=F32)
                m = jnp.max(s, axis=-1, keepdims=True)
                p = jnp.exp2(s - m)
                l = jnp.sum(p, axis=-1, keepdims=True)
                o = jnp.dot(p.astype(BF16), vt_ref[...], preferred_element_type=F32) / l
                acc = jnp.where(sel, o, acc)
            o_ref[0, pl.ds(q0, DENSE_Q), :] = acc.astype(BF16)
        return carry

    lax.fori_loop(0, s_len // (DENSE_Q * DENSE_UNROLL), body, 0)


def _dense_attention(q, k, v):
    b, s, _ = q.shape
    gw = 4 * HEAD_DIM
    rep = _replication_matrices(C_KV_HEADS)
    return pl.pallas_call(
        _dense_kernel, grid=(b, C_KV_HEADS),
        in_specs=[pl.BlockSpec((1, s, gw), lambda bi, g: (bi, 0, g)),
                  pl.BlockSpec((1, s, C_KVW), lambda bi, g: (bi, 0, 0)),
                  pl.BlockSpec((1, s, C_KVW), lambda bi, g: (bi, 0, 0)),
                  pl.BlockSpec((1, C_KVW, gw), lambda bi, g: (g, 0, 0))],
        out_specs=pl.BlockSpec((1, s, gw), lambda bi, g: (bi, 0, g)),
        out_shape=jax.ShapeDtypeStruct((b, s, C_QW), BF16),
        scratch_shapes=[pltpu.VMEM((s, gw), BF16), pltpu.VMEM((s, gw), BF16)],
        compiler_params=_cparams(("parallel", "arbitrary")), name="dense_attention")(q, k, v, rep)


def _out_proj_kernel(*refs, n_parts):
    it = iter(refs)
    x_ref = next(it)
    parts = [next(it) for _ in range(n_parts)]
    w_ref = next(it)
    gain_ref = next(it)
    wr_ref = next(it)
    xo_ref = next(it)
    hpk_ref = next(it)
    aff_ref = next(it)
    afft_ref = next(it)
    h_ref = next(it)

    y = x_ref[...]
    c0 = 0
    for p_ref in parts:
        wp = p_ref.shape[1]
        y = y + jnp.dot(p_ref[...], w_ref[c0:c0 + wp, :], preferred_element_type=F32)
        c0 += wp
    xo_ref[...] = y
    h = _rms_rows(y) * gain_ref[...]
    h_ref[...] = h
    _to_packed(h_ref, hpk_ref, TM)

    h_hi = h.astype(BF16)
    h_lo = (h - h_hi.astype(F32)).astype(BF16)
    r_hi = jnp.dot(h_hi, wr_ref[...], preferred_element_type=F32)
    r_lo = jnp.dot(h_lo, wr_ref[:, :LANES], preferred_element_type=F32)
    lane = lax.broadcasted_iota(jnp.int32, (1, LANES), 1)
    logits = r_hi[:, :LANES] + r_hi[:, LANES:] + r_lo
    logits = jnp.where(lane < N_EXPERTS, logits, NEG_INF)
    e = jnp.exp(logits - jnp.max(logits, axis=-1, keepdims=True))
    aff = e / jnp.sum(e, axis=-1, keepdims=True)
    aff_ref[...] = aff
    afft_ref[0] = jnp.transpose(aff)[:N_EXPERTS, :]


def _out_proj(x2d, parts, w_out_bf16, ffn_gain, w_router, batch, seq):
    m = x2d.shape[0]
    tiles_per_seq = seq // TM
    wr = w_router.astype(F32)
    wr_hi = wr.astype(BF16)
    wr_lo = (wr - wr_hi.astype(F32)).astype(BF16)
    wr_split = jnp.zeros((D_MODEL, 2 * LANES), BF16)
    wr_split = wr_split.at[:, :N_EXPERTS].set(wr_hi).at[:, LANES:LANES + N_EXPERTS].set(wr_lo)
    in_specs = [pl.BlockSpec((TM, D_MODEL), lambda i: (i, 0))]
    in_specs += [pl.BlockSpec((TM, p.shape[1]), lambda i: (i, 0)) for p in parts]
    in_specs += [pl.BlockSpec((D_MODEL, D_MODEL), lambda i: (0, 0)),
                 pl.BlockSpec((1, D_MODEL), lambda i: (0, 0)),
                 pl.BlockSpec((D_MODEL, 2 * LANES), lambda i: (0, 0))]
    out_shape = [jax.ShapeDtypeStruct((m, D_MODEL), F32),
                 jax.ShapeDtypeStruct((m * PACK_TILES, LANES), U32),
                 jax.ShapeDtypeStruct((m, LANES), F32),
                 jax.ShapeDtypeStruct((batch, N_EXPERTS, seq), F32)]
    out_specs = [pl.BlockSpec((TM, D_MODEL), lambda i: (i, 0)),
                 pl.BlockSpec((TM * PACK_TILES, LANES), lambda i: (i, 0)),
                 pl.BlockSpec((TM, LANES), lambda i: (i, 0)),
                 pl.BlockSpec((1, N_EXPERTS, TM), lambda i: (i // tiles_per_seq, 0, i % tiles_per_seq))]
    kern = functools.partial(_out_proj_kernel, n_parts=len(parts))
    return pl.pallas_call(
        kern, grid=(m // TM,), in_specs=in_specs, out_specs=out_specs, out_shape=out_shape,
        scratch_shapes=[pltpu.VMEM((TM, D_MODEL), F32)],
        compiler_params=_cparams(("parallel",)), name="out_proj_router")(
            x2d, *parts, w_out_bf16, ffn_gain.reshape(1, D_MODEL), wr_split)


def _topk_kernel(afft_ref, aff_ref, ids_ref, gate_ref, res_ref, *, cap):
    s_len = afft_ref.shape[2]
    at = afft_ref[0]
    bits = pltpu.bitcast(at, jnp.int32)

    def count(mask):
        return jnp.sum(jnp.where(mask, 1.0, 0.0), axis=1, keepdims=True)

    def bit_step(k, thr):
        cand = thr | jnp.left_shift(jnp.int32(1), 30 - k)
        return jnp.where(count(bits >= cand) >= cap, cand, thr)

    thr = lax.fori_loop(0, 31, bit_step, jnp.zeros((N_EXPERTS, 1), jnp.int32))
    gt = bits > thr
    eq = bits == thr
    need = cap - count(gt)

    ri = lax.broadcasted_iota(jnp.int32, (LANES, LANES), 0)
    ci = lax.broadcasted_iota(jnp.int32, (LANES, LANES), 1)
    upper = jnp.where(ri < ci, 1.0, 0.0).astype(BF16)

    def prefix(mask_f32):
        out = []
        run = jnp.zeros((N_EXPERTS, 1), F32)
        for c in range(s_len // LANES):
            blk = mask_f32[:, c * LANES:(c + 1) * LANES]
            out.append(jnp.dot(blk.astype(BF16), upper, preferred_element_type=F32) + run)
            run = run + jnp.sum(blk, axis=1, keepdims=True)
        return out

    eq_f = jnp.where(eq, 1.0, 0.0)
    eq_rank = prefix(eq_f)
    sel_tiles = []
    for c in range(s_len // LANES):
        sl = slice(c * LANES, (c + 1) * LANES)
        sel_tiles.append(jnp.where(gt[:, sl] | (eq[:, sl] & (eq_rank[c] < need)), 1.0, 0.0))
    sel_f = jnp.concatenate(sel_tiles, axis=1)
    pos = jnp.concatenate(prefix(sel_f), axis=1)
    key = jnp.where(sel_f > 0.5, pos, -1.0)

    a = aff_ref[0]
    a_hi = a.astype(BF16).astype(F32)
    r1 = a - a_hi
    a_mid = r1.astype(BF16).astype(F32)
    a_lo = r1 - a_mid
    lane = lax.broadcasted_iota(jnp.int32, (s_len, LANES), 1)
    tok = lax.broadcasted_iota(jnp.int32, (s_len, LANES), 0)
    packed = a_hi + pltpu.roll(a_mid, N_EXPERTS, 1) + pltpu.roll(a_lo, 2 * N_EXPERTS, 1)
    packed = jnp.where(lane == 64, (tok // 64).astype(F32), packed)
    packed = jnp.where(lane == 65, (tok % 64).astype(F32), packed)
    rmat = packed.astype(BF16)

    slot = lax.broadcasted_iota(jnp.int32, (cap, 1), 0).astype(F32)
    for e in range(N_EXPERTS):
        onehot_t = jnp.where(key[e:e + 1, :] == slot, 1.0, 0.0).astype(BF16)
        res_ref[...] = jnp.dot(onehot_t, rmat, preferred_element_type=F32)
        res = res_ref[...]
        res_t = jnp.transpose(res)
        ids_ref[0, e:e + 1, :] = (res_t[64:65, :] * 64.0 + res_t[65:66, :]).astype(jnp.int32)
        gate = (res[:, e:e + 1] + res[:, N_EXPERTS + e:N_EXPERTS + e + 1]
                + res[:, 2 * N_EXPERTS + e:2 * N_EXPERTS + e + 1])
        gate_ref[0, e] = jnp.broadcast_to(gate, (cap, LANES))


def _topk(aff_t, aff, cap):
    b, _, s = aff_t.shape
    kern = functools.partial(_topk_kernel, cap=cap)
    return pl.pallas_call(
        kern, grid=(b,),
        in_specs=[pl.BlockSpec((1, N_EXPERTS, s), lambda bi: (bi, 0, 0)),
                  pl.BlockSpec((1, s, LANES), lambda bi: (bi, 0, 0))],
        out_specs=[pl.BlockSpec((1, N_EXPERTS, cap), lambda bi: (bi, 0, 0)),
                   pl.BlockSpec((1, N_EXPERTS, cap, LANES), lambda bi: (bi, 0, 0, 0))],
        out_shape=[jax.ShapeDtypeStruct((b, N_EXPERTS, cap), jnp.int32),
                   jax.ShapeDtypeStruct((b, N_EXPERTS, cap, LANES), F32)],
        scratch_shapes=[pltpu.VMEM((cap, LANES), F32)],
        compiler_params=_cparams(("parallel",)), name="expert_topk")(aff_t, aff)


MOE_GROUP = 16
SCATTER_GROUP = 8
MERGE_ROWS = 1024


MOE_EXPERTS_PER_STEP = 4
FFN_ROWS = 512
FFN_F_CHUNK = 512


def _gather_kernel(ids_ref, h_ref, o_ref, *, cap):
    for el in range(MOE_EXPERTS_PER_STEP):
        def body(jg, carry, el=el):
            for u in range(MOE_GROUP):
                j = jg * MOE_GROUP + u
                o_ref[el, j] = h_ref[0, ids_ref[0, 0, el * cap + j]]
            return carry
        lax.fori_loop(0, cap // MOE_GROUP, body, 0)


def _ffn_kernel(xe_ref, gate_ref, wg_ref, wu_ref, wd_ref, o_ref, xb_ref, wgb_ref, wub_ref, wdb_ref,
                *, n_chunks):
    phase = pl.program_id(0)
    step = pl.program_id(1)
    n_experts = pl.num_programs(0) - 1
    load_slot = phase % 2
    chunk = D_FF // n_chunks
    for c in range(n_chunks):
        @pl.when((step == c) & (phase < n_experts))
        def _(c=c):
            cols = slice(c * chunk, (c + 1) * chunk)
            wgb_ref[load_slot, :, cols] = wg_ref[0, 0].astype(BF16)
            wub_ref[load_slot, :, cols] = wu_ref[0, 0].astype(BF16)
            wdb_ref[load_slot, cols, :] = wd_ref[0, 0].astype(BF16)

    @pl.when(phase == 0)
    def _():
        o_ref[...] = jnp.zeros_like(o_ref)

    @pl.when(phase > 0)
    def _():
        _ffn_step(xe_ref, gate_ref, wgb_ref, wub_ref, wdb_ref, o_ref, xb_ref, 1 - load_slot)


def _ffn_step(xe_ref, gate_ref, wg_ref, wu_ref, wd_ref, o_ref, xb_ref, slot):
    pair = 2 * SUBLANES
    for i in range(FFN_ROWS // pair):
        for c in range(PACK_TILES):
            base = i * pair * PACK_TILES + c
            lo_a, hi_a = _unpack_pair(xe_ref[0, pl.ds(base, SUBLANES, stride=PACK_TILES), :])
            lo_b, hi_b = _unpack_pair(
                xe_ref[0, pl.ds(base + SUBLANES * PACK_TILES, SUBLANES, stride=PACK_TILES), :])
            rows = slice(i * pair, (i + 1) * pair)
            xb_ref[rows, c * LANES:(c + 1) * LANES] = jnp.concatenate([lo_a, lo_b], axis=0).astype(BF16)
            xb_ref[rows, HALF_D + c * LANES:HALF_D + (c + 1) * LANES] = (
                jnp.concatenate([hi_a, hi_b], axis=0).astype(BF16))
    xe = xb_ref[...]
    y = None
    for f0 in range(0, D_FF, FFN_F_CHUNK):
        hg = jnp.dot(xe, wg_ref[slot, :, f0:f0 + FFN_F_CHUNK], preferred_element_type=F32)
        hu = jnp.dot(xe, wu_ref[slot, :, f0:f0 + FFN_F_CHUNK], preferred_element_type=F32)
        hid = (hg * (1.0 / (1.0 + jnp.exp(-hg))) * hu).astype(BF16)
        part = jnp.dot(hid, wd_ref[slot, f0:f0 + FFN_F_CHUNK, :], preferred_element_type=F32)
        y = part if y is None else y + part
    cap = gate_ref.shape[2]
    for i in range(FFN_ROWS // SUBLANES):
        r0 = i * SUBLANES
        gate = gate_ref[r0 // cap, 0, r0 % cap:r0 % cap + SUBLANES, :]
        for c in range(ROW_TILES):
            o_ref[0, pl.ds(r0 * ROW_TILES + c, SUBLANES, stride=ROW_TILES), :] = (
                y[r0:r0 + SUBLANES, c * LANES:(c + 1) * LANES] * gate)


def _scatter_kernel(ids_ref, y_ref, x_ref, o_ref, acc_ref, *, cap, n_eg):
    step = pl.program_id(1)

    @pl.when(step == 0)
    def _():
        acc_ref[...] = jnp.zeros_like(acc_ref)

    @pl.when(step < n_eg)
    def _():
        for el in range(MOE_EXPERTS_PER_STEP):
            def body(jg, carry, el=el):
                rows = []
                for u in range(SCATTER_GROUP):
                    j = jg * SCATTER_GROUP + u
                    rows.append(pl.multiple_of(ids_ref[0, 0, el * cap + j] * ROW_TILES, ROW_TILES))
                vals = []
                for u in range(SCATTER_GROUP):
                    j = jg * SCATTER_GROUP + u
                    src = pl.multiple_of(j * ROW_TILES, ROW_TILES)
                    vals.append(acc_ref[pl.ds(rows[u], ROW_TILES), :] + y_ref[el, pl.ds(src, ROW_TILES), :])
                for u in range(SCATTER_GROUP):
                    acc_ref[pl.ds(rows[u], ROW_TILES), :] = vals[u]
                return carry
            lax.fori_loop(0, cap // SCATTER_GROUP, body, 0)

    @pl.when(step >= n_eg)
    def _():
        first = (step - n_eg) * (MERGE_ROWS * ROW_TILES)

        def body(i, carry):
            r0 = pl.multiple_of(i * SUBLANES, SUBLANES)
            base = first + i * (SUBLANES * ROW_TILES)
            for c in range(ROW_TILES):
                cols = slice(c * LANES, (c + 1) * LANES)
                o_ref[pl.ds(r0, SUBLANES), cols] = (
                    x_ref[pl.ds(r0, SUBLANES), cols]
                    + acc_ref[pl.ds(base + c, SUBLANES, stride=ROW_TILES), :])
            return carry
        lax.fori_loop(0, MERGE_ROWS // SUBLANES, body, 0)


def _moe(x2d, ids, gates, h_pk, wg, wu, wd, layer, cap):
    b = ids.shape[0]
    seq = h_pk.shape[0] // (b * PACK_TILES)
    n_eg = N_EXPERTS // MOE_EXPERTS_PER_STEP
    step_ids = MOE_EXPERTS_PER_STEP * cap
    ids_g = ids.reshape(b * n_eg, 1, step_ids)
    smem_spec = pl.BlockSpec((1, 1, step_ids), lambda bi, eg: (bi * n_eg + eg, 0, 0), memory_space=pltpu.SMEM)

    xe_pk = pl.pallas_call(
        functools.partial(_gather_kernel, cap=cap), grid=(b, n_eg),
        in_specs=[smem_spec, pl.BlockSpec((1, seq, PACK_TILES, LANES), lambda bi, eg: (bi, 0, 0, 0))],
        out_specs=pl.BlockSpec((MOE_EXPERTS_PER_STEP, cap, PACK_TILES, LANES), lambda bi, eg: (eg, bi, 0, 0)),
        out_shape=jax.ShapeDtypeStruct((N_EXPERTS, b * cap, PACK_TILES, LANES), U32),
        compiler_params=_cparams(("parallel", "arbitrary")), name="moe_gather")(
            ids_g, h_pk.reshape(b, seq, PACK_TILES, LANES))

    assert (b * cap) % FFN_ROWS == 0 and FFN_ROWS % cap == 0
    slot_rows = cap * ROW_TILES
    seqs_per_step = FFN_ROWS // cap
    n_tiles = b * cap // FFN_ROWS
    chunk = D_FF // n_tiles
    assert chunk * n_tiles == D_FF and chunk % LANES == 0

    def cur(p):
        return jnp.maximum(p - 1, 0)

    def nxt(p):
        return jnp.minimum(p, N_EXPERTS - 1)

    ye_rt = pl.pallas_call(
        functools.partial(_ffn_kernel, n_chunks=n_tiles), grid=(N_EXPERTS + 1, n_tiles),
        in_specs=[pl.BlockSpec((1, FFN_ROWS * PACK_TILES, LANES), lambda p, i: (cur(p), i, 0)),
                  pl.BlockSpec((seqs_per_step, 1, cap, LANES), lambda p, i: (i, cur(p), 0, 0)),
                  pl.BlockSpec((1, 1, D_MODEL, chunk), lambda p, i: (layer, nxt(p), 0, i)),
                  pl.BlockSpec((1, 1, D_MODEL, chunk), lambda p, i: (layer, nxt(p), 0, i)),
                  pl.BlockSpec((1, 1, chunk, D_MODEL), lambda p, i: (layer, nxt(p), i, 0))],
        out_specs=pl.BlockSpec((1, FFN_ROWS * ROW_TILES, LANES), lambda p, i: (cur(p), i, 0)),
        out_shape=jax.ShapeDtypeStruct((N_EXPERTS, b * slot_rows, LANES), F32),
        scratch_shapes=[pltpu.VMEM((FFN_ROWS, D_MODEL), BF16),
                        pltpu.VMEM((2, D_MODEL, D_FF), BF16), pltpu.VMEM((2, D_MODEL, D_FF), BF16),
                        pltpu.VMEM((2, D_FF, D_MODEL), BF16)],
        compiler_params=_cparams(("arbitrary", "arbitrary")), name="moe_ffn")(
            xe_pk.reshape(N_EXPERTS, b * cap * PACK_TILES, LANES), gates, wg, wu, wd)

    assert seq % MERGE_ROWS == 0
    n_merge = seq // MERGE_ROWS

    def eg_of(st):
        return jnp.minimum(st, n_eg - 1)

    def rows_of(bi, st):
        return (bi * n_merge + jnp.maximum(st - n_eg, 0), 0)

    smem_step = pl.BlockSpec((1, 1, step_ids), lambda bi, st: (bi * n_eg + eg_of(st), 0, 0),
                             memory_space=pltpu.SMEM)
    return pl.pallas_call(
        functools.partial(_scatter_kernel, cap=cap, n_eg=n_eg), grid=(b, n_eg + n_merge),
        in_specs=[smem_step,
                  pl.BlockSpec((MOE_EXPERTS_PER_STEP, slot_rows, LANES), lambda bi, st: (eg_of(st), bi, 0)),
                  pl.BlockSpec((MERGE_ROWS, D_MODEL), rows_of)],
        out_specs=pl.BlockSpec((MERGE_ROWS, D_MODEL), rows_of),
        out_shape=jax.ShapeDtypeStruct((b * seq, D_MODEL), F32),
        scratch_shapes=[pltpu.VMEM((seq * ROW_TILES, LANES), F32)],
        compiler_params=_cparams(("parallel", "arbitrary")), name="moe_scatter_merge")(
            ids_g, ye_rt, x2d)


def _rope_tables(seq):
    t = jnp.arange(seq).astype(F32)
    inv = jnp.power(jnp.float32(ROPE_THETA), -jnp.arange(0, HEAD_DIM, 2, dtype=F32) / HEAD_DIM)
    ang = t[:, None] * inv[None, :]
    cos = jnp.concatenate([jnp.cos(ang), jnp.cos(ang)], axis=1)
    sin = jnp.concatenate([-jnp.sin(ang), jnp.sin(ang)], axis=1)
    return jnp.tile(cos, (1, 4)), jnp.tile(sin, (1, 4))


def _axial_tables(seq):
    t = jnp.arange(seq)
    half = HEAD_DIM // 2
    inv = jnp.power(jnp.float32(ROPE_THETA), -jnp.arange(0, half, 2, dtype=F32) / half)
    ang_r = (t // GRID_W).astype(F32)[:, None] * inv[None, :]
    ang_c = (t % GRID_W).astype(F32)[:, None] * inv[None, :]
    cos = jnp.concatenate([jnp.cos(ang_r), jnp.cos(ang_r), jnp.cos(ang_c), jnp.cos(ang_c)], axis=1)
    sin = jnp.concatenate([-jnp.sin(ang_r), jnp.sin(ang_r), -jnp.sin(ang_c), jnp.sin(ang_c)], axis=1)
    return jnp.tile(cos, (1, 4)), jnp.tile(sin, (1, 4))


def _col_gain(pieces):
    cols = []
    for gain, n_heads, scale in pieces:
        if gain is None:
            cols.append(jnp.ones((n_heads * HEAD_DIM,), F32))
        else:
            cols.append(jnp.tile(gain.astype(F32) * scale, n_heads))
    return jnp.concatenate(cols)[None, :]


def _chunks(start, width, kind, oi):
    out = []
    step = 256 if width % 256 == 0 else LANES
    for c in range(0, width, step):
        out.append((start + c, step, kind, oi, c))
    return out


def _moe_block(x2d, parts, w_out, ffn_gain, w_router, wg, wu, wd, layer, batch, seq):
    cap = EC_CAPACITY_FACTOR * seq // N_EXPERTS
    x_new, h_pk, aff, aff_t = _out_proj(x2d, parts, w_out.astype(BF16), ffn_gain, w_router, batch, seq)
    ids, gates = _topk(aff_t, aff.reshape(batch, seq, LANES), cap)
    return _moe(x_new, ids, gates, h_pk, wg, wu, wd, layer, cap)


def kernel(x, attn_norm_even, w_in_even, q_norm_a, k_norm_a, rel_bias_a, q_norm_b, k_norm_b, sink_b,
           w_out_even, attn_norm_odd, w_in_odd, q_norm_c, k_norm_c, w_out_odd,
           ffn_norm, w_router, w_gate, w_up, w_down):
    batch, seq, _ = x.shape
    m = batch * seq
    scale = HEAD_DIM ** -0.5 * LOG2E
    x2d = x.reshape(m, D_MODEL)
    depth = ffn_norm.shape[0]
    for layer in range(depth):
        i = layer // 2
        if layer % 2 == 0:
            cos_t, sin_t = _rope_tables(seq)
            colgain = _col_gain([(q_norm_a[i], A_HEADS, scale), (k_norm_a[i], A_HEADS, 1.0),
                                 (None, A_HEADS, 1.0), (q_norm_b[i], B_HEADS, scale),
                                 (k_norm_b[i], B_KV_HEADS, 1.0), (None, B_KV_HEADS, 1.0)])
            sections = (_chunks(0, A_W, "norm", 0) + _chunks(A_W, A_W, "norm", 1)
                        + _chunks(2 * A_W, A_W, "plain", 2) + _chunks(3 * A_W, B_QW, "rope", 3)
                        + _chunks(3 * A_W + B_QW, B_KVW, "rope", 4)
                        + _chunks(3 * A_W + B_QW + B_KVW, B_KVW, "plain", 5))
            widths = [A_W, A_W, A_W, B_QW, B_KVW, B_KVW]
            outs = _norm_proj(x2d, attn_norm_even[i], w_in_even[i].astype(BF16), colgain,
                              cos_t, sin_t, sections, widths, HEAD_DIM // 2, seq)
            qa, ka, va, qb, kb, vb = [o.reshape(batch, seq, -1) for o in outs]
            out_a = _na_attention(qa, ka, va, rel_bias_a[i])
            out_b = _swa_attention(qb, kb, vb, sink_b[i])
            parts = [out_a.reshape(m, A_W), out_b.reshape(m, B_QW)]
            w_out = w_out_even[i]
        else:
            cos_t, sin_t = _axial_tables(seq)
            colgain = _col_gain([(q_norm_c[i], C_HEADS, scale), (k_norm_c[i], C_KV_HEADS, 1.0),
                                 (None, C_KV_HEADS, 1.0)])
            sections = (_chunks(0, C_QW, "rope", 0) + _chunks(C_QW, C_KVW, "rope", 1)
                        + _chunks(C_QW + C_KVW, C_KVW, "plain", 2))
            widths = [C_QW, C_KVW, C_KVW]
            outs = _norm_proj(x2d, attn_norm_odd[i], w_in_odd[i].astype(BF16), colgain,
                              cos_t, sin_t, sections, widths, HEAD_DIM // 4, seq)
            qc, kc, vc = [o.reshape(batch, seq, -1) for o in outs]
            parts = [_dense_attention(qc, kc, vc).reshape(m, C_QW)]
            w_out = w_out_odd[i]
        x2d = _moe_block(x2d, parts, w_out, ffn_norm[layer], w_router[layer],
                         w_gate, w_up, w_down, layer, batch, seq)
    return x2d.reshape(batch, seq, D_MODEL)
```

```python
import functools

import numpy as np
import jax
import jax.numpy as jnp
from jax import lax
from jax.experimental import pallas as pl
from jax.experimental.pallas import tpu as pltpu

D_MODEL = 1024
HEAD_DIM = 64
GRID_W = 64
ROPE_THETA = 10000.0
EPS = 1e-6
NEG_INF = -1e30
LOG2E = 1.4426950408889634

A_HEADS = 8
NA_WIN_ROWS = 8
NA_WIN_COLS = 16
B_HEADS = 8
B_KV_HEADS = 2
B_WINDOW = 128
C_HEADS = 16
C_KV_HEADS = 4
N_EXPERTS = 16
EC_CAPACITY_FACTOR = 2
D_FF = 2048

A_W = A_HEADS * HEAD_DIM
B_QW = B_HEADS * HEAD_DIM
B_KVW = B_KV_HEADS * HEAD_DIM
C_QW = C_HEADS * HEAD_DIM
C_KVW = C_KV_HEADS * HEAD_DIM

SUBLANES = 8
LANES = 128
ROW_TILES = D_MODEL // LANES
VMEM_LIMIT = 56 * 1024 * 1024

TM = 512
NA_QROWS = 4
NA_KROWS = 12
SWA_Q = 256
SWA_K = 512
DENSE_Q = 256
DENSE_UNROLL = 8

F32 = jnp.float32
BF16 = jnp.bfloat16


def _cparams(sem):
    return pltpu.CompilerParams(dimension_semantics=sem, vmem_limit_bytes=VMEM_LIMIT)


def _rms_rows(x):
    return x * lax.rsqrt(jnp.mean(x * x, axis=-1, keepdims=True) + EPS)


PACK_TILES = ROW_TILES // 2
HALF_D = D_MODEL // 2
U32 = jnp.uint32


def _pack_pair(lo, hi):
    lo_bits = pltpu.bitcast(lo.astype(BF16).astype(F32), U32) >> 16
    hi_bits = pltpu.bitcast(hi.astype(BF16).astype(F32), U32) & jnp.uint32(0xFFFF0000)
    return lo_bits | hi_bits


def _unpack_pair(words):
    lo = pltpu.bitcast(words << 16, F32)
    hi = pltpu.bitcast(words & jnp.uint32(0xFFFF0000), F32)
    return lo, hi


def _to_packed(src_ref, dst_ref, rows):
    def body(i, carry):
        r0 = pl.multiple_of(i * SUBLANES, SUBLANES)
        base = i * (SUBLANES * PACK_TILES)
        for c in range(PACK_TILES):
            lo = src_ref[pl.ds(r0, SUBLANES), c * LANES:(c + 1) * LANES]
            hi = src_ref[pl.ds(r0, SUBLANES), HALF_D + c * LANES:HALF_D + (c + 1) * LANES]
            dst_ref[pl.ds(base + c, SUBLANES, stride=PACK_TILES), :] = _pack_pair(lo, hi)
        return carry
    lax.fori_loop(0, rows // SUBLANES, body, 0)


def _norm_proj_kernel(x_ref, gain_ref, w_ref, cg_ref, cos_ref, sin_ref, bd_ref, *outs, sections, half):
    hb = (_rms_rows(x_ref[...]) * gain_ref[...]).astype(BF16)

    ys = [jnp.dot(hb, w_ref[:, c0:c0 + width], preferred_element_type=F32)
          for (c0, width, _, _, _) in sections]
    sss = []
    for y, (c0, width, kind, _, _) in zip(ys, sections):
        if kind == "plain":
            sss.append(None)
            continue
        y2 = y * y
        hi = y2.astype(BF16)
        lo = (y2 - hi.astype(F32)).astype(BF16)
        bd = bd_ref[:width, :width]
        sss.append(jnp.dot(hi, bd, preferred_element_type=F32)
                   + jnp.dot(lo, bd, preferred_element_type=F32))
    for y, ss, (c0, width, kind, oi, oc) in zip(ys, sss, sections):
        if kind != "plain":
            y = y * lax.rsqrt(ss + EPS) * cg_ref[:, c0:c0 + width]
        if kind == "rope":
            lane = lax.broadcasted_iota(jnp.int32, (1, width), 1)
            first = (lane % (2 * half)) < half
            part = jnp.where(first, pltpu.roll(y, width - half, 1), pltpu.roll(y, half, 1))
            y = y * cos_ref[:, :width] + part * sin_ref[:, :width]
        outs[oi][:, oc:oc + width] = y.astype(BF16)


def _norm_proj(x2d, gain, w_bf16, colgain, cos_t, sin_t, sections, out_widths, half, seq):
    m = x2d.shape[0]
    n_in = w_bf16.shape[1]
    bd = jnp.asarray(np.kron(np.eye(4), np.full((HEAD_DIM, HEAD_DIM), 1.0 / HEAD_DIM)), BF16)
    tiles_per_seq = seq // TM
    in_specs = [
        pl.BlockSpec((TM, D_MODEL), lambda i: (i, 0)),
        pl.BlockSpec((1, D_MODEL), lambda i: (0, 0)),
        pl.BlockSpec((D_MODEL, n_in), lambda i: (0, 0)),
        pl.BlockSpec((1, n_in), lambda i: (0, 0)),
        pl.BlockSpec((TM, 256), lambda i: (i % tiles_per_seq, 0)),
        pl.BlockSpec((TM, 256), lambda i: (i % tiles_per_seq, 0)),
        pl.BlockSpec((256, 256), lambda i: (0, 0)),
    ]
    out_shape = [jax.ShapeDtypeStruct((m, wd), BF16) for wd in out_widths]
    out_specs = [pl.BlockSpec((TM, wd), lambda i: (i, 0)) for wd in out_widths]
    kern = functools.partial(_norm_proj_kernel, sections=tuple(sections), half=half)
    return pl.pallas_call(
        kern, grid=(m // TM,), in_specs=in_specs, out_specs=out_specs, out_shape=out_shape,
        compiler_params=_cparams(("parallel",)), name="norm_proj")(
            x2d, gain.reshape(1, D_MODEL), w_bf16, colgain, cos_t, sin_t, bd)


def _na_kernel(q_ref, k_ref, v_ref, bias_ref, o_ref):
    nq = NA_QROWS * GRID_W
    nk = NA_KROWS * GRID_W
    rows = q_ref.shape[1] // GRID_W
    n_blocks = rows // NA_QROWS
    lane = lax.broadcasted_iota(jnp.int32, (1, LANES), 1)
    head0 = lane < HEAD_DIM

    edge_keys = NA_WIN_ROWS * GRID_W
    for qb in range(n_blocks):
        r0 = qb * NA_QROWS
        start = min(max(r0 - NA_WIN_ROWS // 2, 0), rows - NA_KROWS)
        pat = 0 if qb == 0 else (2 if qb == n_blocks - 1 else 1)
        c0, c1 = {0: (0, edge_keys), 1: (0, nk), 2: (nk - edge_keys, nk)}[pat]
        qs = r0 * GRID_W
        ks = start * GRID_W + c0
        q = q_ref[0, qs:qs + nq, :]
        k = k_ref[0, ks:ks + (c1 - c0), :]
        v = v_ref[0, ks:ks + (c1 - c0), :]
        acc = None
        for hh in range(2):
            sel = head0 if hh == 0 else jnp.logical_not(head0)
            qm = jnp.where(sel, q, jnp.zeros_like(q))
            s = lax.dot_general(qm, k, (((1,), (1,)), ((), ())), preferred_element_type=F32)
            s = s + bias_ref[pat, hh, :, c0:c1]
            m = jnp.max(s, axis=-1, keepdims=True)
            p = jnp.exp2(s - m)
            l = jnp.sum(p, axis=-1, keepdims=True)
            o = jnp.dot(p.astype(BF16), v, preferred_element_type=F32) / l
            acc = o if acc is None else jnp.where(sel, o, acc)
        o_ref[0, qs:qs + nq, :] = acc.astype(BF16)


def _na_bias_tables(rel_bias, rows):
    n_dr, n_dc = 2 * NA_WIN_ROWS - 1, 2 * NA_WIN_COLS - 1
    sel_r = np.zeros((3, NA_QROWS, NA_KROWS, n_dr), np.float32)
    in_r = np.zeros((3, NA_QROWS, NA_KROWS), np.float32)
    for p, r0 in enumerate((0, 2 * NA_QROWS, rows - NA_QROWS)):
        start = int(np.clip(r0 - NA_WIN_ROWS // 2, 0, rows - NA_KROWS))
        for lr in range(NA_QROWS):
            r = r0 + lr
            rs = int(np.clip(r - NA_WIN_ROWS // 2, 0, rows - NA_WIN_ROWS))
            for kl in range(NA_KROWS):
                kr = start + kl
                if rs <= kr < rs + NA_WIN_ROWS:
                    in_r[p, lr, kl] = 1.0
                    sel_r[p, lr, kl, kr - r + NA_WIN_ROWS - 1] = 1.0
    sel_c = np.zeros((GRID_W, GRID_W, n_dc), np.float32)
    in_c = np.zeros((GRID_W, GRID_W), np.float32)
    for c in range(GRID_W):
        cs = int(np.clip(c - NA_WIN_COLS // 2, 0, GRID_W - NA_WIN_COLS))
        for kc in range(cs, cs + NA_WIN_COLS):
            in_c[c, kc] = 1.0
            dc = int(np.clip(kc - c, -(NA_WIN_COLS - 1), NA_WIN_COLS - 1))
            sel_c[c, kc, dc + NA_WIN_COLS - 1] = 1.0
    hp = lax.Precision.HIGHEST
    rows_sel = jnp.einsum('plkd,hde->phlke', jnp.asarray(sel_r), rel_bias.astype(F32), precision=hp)
    vals = jnp.einsum('phlke,cje->phlckj', rows_sel, jnp.asarray(sel_c), precision=hp)
    inside = jnp.asarray(in_r[:, None, :, None, :, None] * in_c[None, None, None, :, None, :])
    tab = jnp.where(inside > 0.5, vals * LOG2E, NEG_INF)
    return tab.reshape(3, rel_bias.shape[0], NA_QROWS * GRID_W, NA_KROWS * GRID_W)


def _na_attention(q, k, v, rel_bias):
    b, s, _ = q.shape
    rows = s // GRID_W
    assert rows % NA_QROWS == 0 and rows >= NA_KROWS and rows // NA_QROWS >= 3
    assert NA_QROWS <= NA_WIN_ROWS // 2
    nq, nk = NA_QROWS * GRID_W, NA_KROWS * GRID_W
    bias = _na_bias_tables(rel_bias, rows)
    blk = pl.BlockSpec((1, s, LANES), lambda bi, hp: (bi, 0, hp))
    return pl.pallas_call(
        _na_kernel, grid=(b, A_HEADS // 2),
        in_specs=[blk, blk, blk, pl.BlockSpec((3, 2, nq, nk), lambda bi, hp: (0, hp, 0, 0))],
        out_specs=blk, out_shape=jax.ShapeDtypeStruct((b, s, A_W), BF16),
        compiler_params=_cparams(("parallel", "arbitrary")), name="na_attention")(q, k, v, bias)


def _swa_kernel(sink_ref, q_ref, k_ref, v_ref, rep_ref, o_ref, kt_ref, vt_ref):
    s_len = q_ref.shape[1]
    n_groups = B_HEADS // B_KV_HEADS
    gw = n_groups * HEAD_DIM
    lane = lax.broadcasted_iota(jnp.int32, (1, gw), 1)
    rel = (lax.broadcasted_iota(jnp.int32, (SWA_Q, SWA_K), 0)
           - lax.broadcasted_iota(jnp.int32, (SWA_Q, SWA_K), 1))

    for g in range(B_KV_HEADS):
        kt_ref[...] = jnp.dot(k_ref[0], rep_ref[g], preferred_element_type=F32).astype(BF16)
        vt_ref[...] = jnp.dot(v_ref[0], rep_ref[g], preferred_element_type=F32).astype(BF16)

        for i in range(s_len // SWA_Q):
            q0 = i * SWA_Q
            k0 = min(max(q0 - B_WINDOW, 0), s_len - SWA_K)
            q = q_ref[0, q0:q0 + SWA_Q, g * gw:(g + 1) * gw]
            kt = kt_ref[k0:k0 + SWA_K, :]
            vt = vt_ref[k0:k0 + SWA_K, :]
            valid = jnp.abs(rel + (q0 - k0)) <= B_WINDOW
            acc = jnp.zeros((SWA_Q, gw), F32)
            for j in range(n_groups):
                sel = (lane >= j * HEAD_DIM) & (lane < (j + 1) * HEAD_DIM)
                sink = sink_ref[g * n_groups + j]
                qm = jnp.where(sel, q, jnp.zeros_like(q))
                s = lax.dot_general(qm, kt, (((1,), (1,)), ((), ())), preferred_element_type=F32)
                s = jnp.where(valid, s, NEG_INF)
                m = jnp.maximum(jnp.max(s, axis=-1, keepdims=True), sink)
                e = jnp.exp2(s - m)
                den = jnp.sum(e, axis=-1, keepdims=True) + jnp.exp2(sink - m)
                o = jnp.dot(e.astype(BF16), vt, preferred_element_type=F32) / den
                acc = jnp.where(sel, o, acc)
            o_ref[0, q0:q0 + SWA_Q, g * gw:(g + 1) * gw] = acc.astype(BF16)


def _replication_matrices(n_kv):
    rep = np.zeros((n_kv, n_kv * HEAD_DIM, 4 * HEAD_DIM), np.float32)
    for g in range(n_kv):
        for j in range(4 * HEAD_DIM):
            rep[g, g * HEAD_DIM + j % HEAD_DIM, j] = 1.0
    return jnp.asarray(rep, BF16)


def _swa_attention(q, k, v, sink):
    b, s, _ = q.shape
    assert s % SWA_Q == 0 and s >= SWA_K and SWA_K >= SWA_Q + 2 * B_WINDOW
    rep = _replication_matrices(B_KV_HEADS)
    return pl.pallas_call(
        _swa_kernel, grid=(b,),
        in_specs=[pl.BlockSpec(memory_space=pltpu.SMEM),
                  pl.BlockSpec((1, s, B_QW), lambda bi: (bi, 0, 0)),
                  pl.BlockSpec((1, s, B_KVW), lambda bi: (bi, 0, 0)),
                  pl.BlockSpec((1, s, B_KVW), lambda bi: (bi, 0, 0)),
                  pl.BlockSpec((B_KV_HEADS, B_KVW, 4 * HEAD_DIM), lambda bi: (0, 0, 0))],
        out_specs=pl.BlockSpec((1, s, B_QW), lambda bi: (bi, 0, 0)),
        out_shape=jax.ShapeDtypeStruct((b, s, B_QW), BF16),
        scratch_shapes=[pltpu.VMEM((s, 4 * HEAD_DIM), BF16), pltpu.VMEM((s, 4 * HEAD_DIM), BF16)],
        compiler_params=_cparams(("parallel",)), name="swa_attention")(sink.astype(F32) * LOG2E, q, k, v, rep)


def _dense_kernel(q_ref, k_ref, v_ref, rep_ref, o_ref, kt_ref, vt_ref):
    s_len = q_ref.shape[1]
    gw = 4 * HEAD_DIM
    lane = lax.broadcasted_iota(jnp.int32, (1, gw), 1)
    kt_ref[...] = jnp.dot(k_ref[0], rep_ref[0], preferred_element_type=F32).astype(BF16)
    vt_ref[...] = jnp.dot(v_ref[0], rep_ref[0], preferred_element_type=F32).astype(BF16)

    def body(i, carry):
        for u in range(DENSE_UNROLL):
            q0 = pl.multiple_of((i * DENSE_UNROLL + u) * DENSE_Q, DENSE_Q)
            q = q_ref[0, pl.ds(q0, DENSE_Q), :]
            acc = jnp.zeros((DENSE_Q, gw), F32)
            for j in range(4):
                sel = (lane >= j * HEAD_DIM) & (lane < (j + 1) * HEAD_DIM)
                qm = jnp.where(sel, q, jnp.zeros_like(q))
                s = lax.dot_general(qm, kt_ref[...], (((1,), (1,)), ((), ())), preferred_element_type=F32)
                m = jnp.max(s, axis=-1, keepdims=True)
                p = jnp.exp2(s - m)
                l = jnp.sum(p, axis=-1, keepdims=True)
                o = jnp.dot(p.astype(BF16), vt_ref[...], preferred_element_type=F32) / l
                acc = jnp.where(sel, o, acc)
            o_ref[0, pl.ds(q0, DENSE_Q), :] = acc.astype(BF16)
        return carry

    lax.fori_loop(0, s_len // (DENSE_Q * DENSE_UNROLL), body, 0)


def _dense_attention(q, k, v):
    b, s, _ = q.shape
    gw = 4 * HEAD_DIM
    rep = _replication_matrices(C_KV_HEADS)
    return pl.pallas_call(
        _dense_kernel, grid=(b, C_KV_HEADS),
        in_specs=[pl.BlockSpec((1, s, gw), lambda bi, g: (bi, 0, g)),
                  pl.BlockSpec((1, s, C_KVW), lambda bi, g: (bi, 0, 0)),
                  pl.BlockSpec((1, s, C_KVW), lambda bi, g: (bi, 0, 0)),
                  pl.BlockSpec((1, C_KVW, gw), lambda bi, g: (g, 0, 0))],
        out_specs=pl.BlockSpec((1, s, gw), lambda bi, g: (bi, 0, g)),
        out_shape=jax.ShapeDtypeStruct((b, s, C_QW), BF16),
        scratch_shapes=[pltpu.VMEM((s, gw), BF16), pltpu.VMEM((s, gw), BF16)],
        compiler_params=_cparams(("parallel", "arbitrary")), name="dense_attention")(q, k, v, rep)


def _out_proj_kernel(*refs, n_parts):
    it = iter(refs)
    x_ref = next(it)
    parts = [next(it) for _ in range(n_parts)]
    w_ref = next(it)
    gain_ref = next(it)
    wr_ref = next(it)
    xo_ref = next(it)
    hpk_ref = next(it)
    aff_ref = next(it)
    afft_ref = next(it)
    h_ref = next(it)

    y = x_ref[...]
    c0 = 0
    for p_ref in parts:
        wp = p_ref.shape[1]
        y = y + jnp.dot(p_ref[...], w_ref[c0:c0 + wp, :], preferred_element_type=F32)
        c0 += wp
    xo_ref[...] = y
    h = _rms_rows(y) * gain_ref[...]
    h_ref[...] = h
    _to_packed(h_ref, hpk_ref, TM)

    h_hi = h.astype(BF16)
    h_lo = (h - h_hi.astype(F32)).astype(BF16)
    r_hi = jnp.dot(h_hi, wr_ref[...], preferred_element_type=F32)
    r_lo = jnp.dot(h_lo, wr_ref[:, :LANES], preferred_element_type=F32)
    lane = lax.broadcasted_iota(jnp.int32, (1, LANES), 1)
    logits = r_hi[:, :LANES] + r_hi[:, LANES:] + r_lo
    logits = jnp.where(lane < N_EXPERTS, logits, NEG_INF)
    e = jnp.exp(logits - jnp.max(logits, axis=-1, keepdims=True))
    aff = e / jnp.sum(e, axis=-1, keepdims=True)
    aff_ref[...] = aff
    afft_ref[0] = jnp.transpose(aff)[:N_EXPERTS, :]


def _out_proj(x2d, parts, w_out_bf16, ffn_gain, w_router, batch, seq):
    m = x2d.shape[0]
    tiles_per_seq = seq // TM
    wr = w_router.astype(F32)
    wr_hi = wr.astype(BF16)
    wr_lo = (wr - wr_hi.astype(F32)).astype(BF16)
    wr_split = jnp.zeros((D_MODEL, 2 * LANES), BF16)
    wr_split = wr_split.at[:, :N_EXPERTS].set(wr_hi).at[:, LANES:LANES + N_EXPERTS].set(wr_lo)
    in_specs = [pl.BlockSpec((TM, D_MODEL), lambda i: (i, 0))]
    in_specs += [pl.BlockSpec((TM, p.shape[1]), lambda i: (i, 0)) for p in parts]
    in_specs += [pl.BlockSpec((D_MODEL, D_MODEL), lambda i: (0, 0)),
                 pl.BlockSpec((1, D_MODEL), lambda i: (0, 0)),
                 pl.BlockSpec((D_MODEL, 2 * LANES), lambda i: (0, 0))]
    out_shape = [jax.ShapeDtypeStruct((m, D_MODEL), F32),
                 jax.ShapeDtypeStruct((m * PACK_TILES, LANES), U32),
                 jax.ShapeDtypeStruct((m, LANES), F32),
                 jax.ShapeDtypeStruct((batch, N_EXPERTS, seq), F32)]
    out_specs = [pl.BlockSpec((TM, D_MODEL), lambda i: (i, 0)),
                 pl.BlockSpec((TM * PACK_TILES, LANES), lambda i: (i, 0)),
                 pl.BlockSpec((TM, LANES), lambda i: (i, 0)),
                 pl.BlockSpec((1, N_EXPERTS, TM), lambda i: (i // tiles_per_seq, 0, i % tiles_per_seq))]
    kern = functools.partial(_out_proj_kernel, n_parts=len(parts))
    return pl.pallas_call(
        kern, grid=(m // TM,), in_specs=in_specs, out_specs=out_specs, out_shape=out_shape,
        scratch_shapes=[pltpu.VMEM((TM, D_MODEL), F32)],
        compiler_params=_cparams(("parallel",)), name="out_proj_router")(
            x2d, *parts, w_out_bf16, ffn_gain.reshape(1, D_MODEL), wr_split)


def _topk_kernel(afft_ref, aff_ref, ids_ref, gate_ref, res_ref, *, cap):
    s_len = afft_ref.shape[2]
    at = afft_ref[0]
    bits = pltpu.bitcast(at, jnp.int32)

    def count(mask):
        return jnp.sum(jnp.where(mask, 1.0, 0.0), axis=1, keepdims=True)

    def bit_step(k, thr):
        cand = thr | jnp.left_shift(jnp.int32(1), 30 - k)
        return jnp.where(count(bits >= cand) >= cap, cand, thr)

    thr = lax.fori_loop(0, 31, bit_step, jnp.zeros((N_EXPERTS, 1), jnp.int32))
    gt = bits > thr
    eq = bits == thr
    need = cap - count(gt)

    ri = lax.broadcasted_iota(jnp.int32, (LANES, LANES), 0)
    ci = lax.broadcasted_iota(jnp.int32, (LANES, LANES), 1)
    upper = jnp.where(ri < ci, 1.0, 0.0).astype(BF16)

    def prefix(mask_f32):
        out = []
        run = jnp.zeros((N_EXPERTS, 1), F32)
        for c in range(s_len // LANES):
            blk = mask_f32[:, c * LANES:(c + 1) * LANES]
            out.append(jnp.dot(blk.astype(BF16), upper, preferred_element_type=F32) + run)
            run = run + jnp.sum(blk, axis=1, keepdims=True)
        return out

    eq_f = jnp.where(eq, 1.0, 0.0)
    eq_rank = prefix(eq_f)
    sel_tiles = []
    for c in range(s_len // LANES):
        sl = slice(c * LANES, (c + 1) * LANES)
        sel_tiles.append(jnp.where(gt[:, sl] | (eq[:, sl] & (eq_rank[c] < need)), 1.0, 0.0))
    sel_f = jnp.concatenate(sel_tiles, axis=1)
    pos = jnp.concatenate(prefix(sel_f), axis=1)
    key = jnp.where(sel_f > 0.5, pos, -1.0)

    a = aff_ref[0]
    a_hi = a.astype(BF16).astype(F32)
    r1 = a - a_hi
    a_mid = r1.astype(BF16).astype(F32)
    a_lo = r1 - a_mid
    lane = lax.broadcasted_iota(jnp.int32, (s_len, LANES), 1)
    tok = lax.broadcasted_iota(jnp.int32, (s_len, LANES), 0)
    packed = a_hi + pltpu.roll(a_mid, N_EXPERTS, 1) + pltpu.roll(a_lo, 2 * N_EXPERTS, 1)
    packed = jnp.where(lane == 64, (tok // 64).astype(F32), packed)
    packed = jnp.where(lane == 65, (tok % 64).astype(F32), packed)
    rmat = packed.astype(BF16)

    slot = lax.broadcasted_iota(jnp.int32, (cap, 1), 0).astype(F32)
    for e in range(N_EXPERTS):
        onehot_t = jnp.where(key[e:e + 1, :] == slot, 1.0, 0.0).astype(BF16)
        res_ref[...] = jnp.dot(onehot_t, rmat, preferred_element_type=F32)
        res = res_ref[...]
        res_t = jnp.transpose(res)
        ids_ref[0, e:e + 1, :] = (res_t[64:65, :] * 64.0 + res_t[65:66, :]).astype(jnp.int32)
        gate = (res[:, e:e + 1] + res[:, N_EXPERTS + e:N_EXPERTS + e + 1]
                + res[:, 2 * N_EXPERTS + e:2 * N_EXPERTS + e + 1])
        gate_ref[0, e] = jnp.broadcast_to(gate, (cap, LANES))


def _topk(aff_t, aff, cap):
    b, _, s = aff_t.shape
    kern = functools.partial(_topk_kernel, cap=cap)
    return pl.pallas_call(
        kern, grid=(b,),
        in_specs=[pl.BlockSpec((1, N_EXPERTS, s), lambda bi: (bi, 0, 0)),
                  pl.BlockSpec((1, s, LANES), lambda bi: (bi, 0, 0))],
        out_specs=[pl.BlockSpec((1, N_EXPERTS, cap), lambda bi: (bi, 0, 0)),
                   pl.BlockSpec((1, N_EXPERTS, cap, LANES), lambda bi: (bi, 0, 0, 0))],
        out_shape=[jax.ShapeDtypeStruct((b, N_EXPERTS, cap), jnp.int32),
                   jax.ShapeDtypeStruct((b, N_EXPERTS, cap, LANES), F32)],
        scratch_shapes=[pltpu.VMEM((cap, LANES), F32)],
        compiler_params=_cparams(("parallel",)), name="expert_topk")(aff_t, aff)


MOE_GROUP = 16
SCATTER_GROUP = 8
MERGE_ROWS = 1024


MOE_EXPERTS_PER_STEP = 4
FFN_ROWS = 512
FFN_F_CHUNK = 512


def _gather_kernel(ids_ref, h_ref, o_ref, *, cap):
    for el in range(MOE_EXPERTS_PER_STEP):
        def body(jg, carry, el=el):
            for u in range(MOE_GROUP):
                j = jg * MOE_GROUP + u
                o_ref[el, j] = h_ref[0, ids_ref[0, 0, el * cap + j]]
            return carry
        lax.fori_loop(0, cap // MOE_GROUP, body, 0)


def _ffn_kernel(xe_ref, gate_ref, wg_ref, wu_ref, wd_ref, o_ref, xb_ref, wgb_ref, wub_ref, wdb_ref,
                *, n_chunks):
    phase = pl.program_id(0)
    step = pl.program_id(1)
    n_experts = pl.num_programs(0) - 1
    load_slot = phase % 2
    chunk = D_FF // n_chunks
    for c in range(n_chunks):
        @pl.when((step == c) & (phase < n_experts))
        def _(c=c):
            cols = slice(c * chunk, (c + 1) * chunk)
            wgb_ref[load_slot, :, cols] = wg_ref[0, 0].astype(BF16)
            wub_ref[load_slot, :, cols] = wu_ref[0, 0].astype(BF16)
            wdb_ref[load_slot, cols, :] = wd_ref[0, 0].astype(BF16)

    @pl.when(phase > 0)
    def _():
        _ffn_step(xe_ref, gate_ref, wgb_ref, wub_ref, wdb_ref, o_ref, xb_ref, 1 - load_slot)


def _ffn_step(xe_ref, gate_ref, wg_ref, wu_ref, wd_ref, o_ref, xb_ref, slot):
    pair = 2 * SUBLANES
    for i in range(FFN_ROWS // pair):
        for c in range(PACK_TILES):
            base = i * pair * PACK_TILES + c
            lo_a, hi_a = _unpack_pair(xe_ref[0, pl.ds(base, SUBLANES, stride=PACK_TILES), :])
            lo_b, hi_b = _unpack_pair(
                xe_ref[0, pl.ds(base + SUBLANES * PACK_TILES, SUBLANES, stride=PACK_TILES), :])
            rows = slice(i * pair, (i + 1) * pair)
            xb_ref[rows, c * LANES:(c + 1) * LANES] = jnp.concatenate([lo_a, lo_b], axis=0).astype(BF16)
            xb_ref[rows, HALF_D + c * LANES:HALF_D + (c + 1) * LANES] = (
                jnp.concatenate([hi_a, hi_b], axis=0).astype(BF16))
    xe = xb_ref[...]
    y = None
    for f0 in range(0, D_FF, FFN_F_CHUNK):
        hg = jnp.dot(xe, wg_ref[slot, :, f0:f0 + FFN_F_CHUNK], preferred_element_type=F32)
        hu = jnp.dot(xe, wu_ref[slot, :, f0:f0 + FFN_F_CHUNK], preferred_element_type=F32)
        hid = (hg * (1.0 / (1.0 + jnp.exp(-hg))) * hu).astype(BF16)
        part = jnp.dot(hid, wd_ref[slot, f0:f0 + FFN_F_CHUNK, :], preferred_element_type=F32)
        y = part if y is None else y + part
    cap = gate_ref.shape[2]
    for i in range(FFN_ROWS // SUBLANES):
        r0 = i * SUBLANES
        gate = gate_ref[r0 // cap, 0, r0 % cap:r0 % cap + SUBLANES, :]
        for c in range(ROW_TILES):
            o_ref[0, pl.ds(r0 * ROW_TILES + c, SUBLANES, stride=ROW_TILES), :] = (
                y[r0:r0 + SUBLANES, c * LANES:(c + 1) * LANES] * gate)


def _scatter_kernel(ids_ref, y_ref, x_ref, o_ref, acc_ref, *, cap, n_eg):
    step = pl.program_id(1)

    @pl.when(step == 0)
    def _():
        acc_ref[...] = jnp.zeros_like(acc_ref)

    @pl.when(step < n_eg)
    def _():
        for el in range(MOE_EXPERTS_PER_STEP):
            def body(jg, carry, el=el):
                rows = []
                for u in range(SCATTER_GROUP):
                    j = jg * SCATTER_GROUP + u
                    rows.append(pl.multiple_of(ids_ref[0, 0, el * cap + j] * ROW_TILES, ROW_TILES))
                vals = []
                for u in range(SCATTER_GROUP):
                    j = jg * SCATTER_GROUP + u
                    src = pl.multiple_of(j * ROW_TILES, ROW_TILES)
                    vals.append(acc_ref[pl.ds(rows[u], ROW_TILES), :] + y_ref[el, pl.ds(src, ROW_TILES), :])
                for u in range(SCATTER_GROUP):
                    acc_ref[pl.ds(rows[u], ROW_TILES), :] = vals[u]
                return carry
            lax.fori_loop(0, cap // SCATTER_GROUP, body, 0)

    @pl.when(step >= n_eg)
    def _():
        first = (step - n_eg) * (MERGE_ROWS * ROW_TILES)

        def body(i, carry):
            r0 = pl.multiple_of(i * SUBLANES, SUBLANES)
            base = first + i * (SUBLANES * ROW_TILES)
            for c in range(ROW_TILES):
                cols = slice(c * LANES, (c + 1) * LANES)
                o_ref[pl.ds(r0, SUBLANES), cols] = (
                    x_ref[pl.ds(r0, SUBLANES), cols]
                    + acc_ref[pl.ds(base + c, SUBLANES, stride=ROW_TILES), :])
            return carry
        lax.fori_loop(0, MERGE_ROWS // SUBLANES, body, 0)


def _moe(x2d, ids, gates, h_pk, wg, wu, wd, layer, cap):
    b = ids.shape[0]
    seq = h_pk.shape[0] // (b * PACK_TILES)
    n_eg = N_EXPERTS // MOE_EXPERTS_PER_STEP
    step_ids = MOE_EXPERTS_PER_STEP * cap
    ids_g = ids.reshape(b * n_eg, 1, step_ids)
    smem_spec = pl.BlockSpec((1, 1, step_ids), lambda bi, eg: (bi * n_eg + eg, 0, 0), memory_space=pltpu.SMEM)

    xe_pk = pl.pallas_call(
        functools.partial(_gather_kernel, cap=cap), grid=(b, n_eg),
        in_specs=[smem_spec, pl.BlockSpec((1, seq, PACK_TILES, LANES), lambda bi, eg: (bi, 0, 0, 0))],
        out_specs=pl.BlockSpec((MOE_EXPERTS_PER_STEP, cap, PACK_TILES, LANES), lambda bi, eg: (eg, bi, 0, 0)),
        out_shape=jax.ShapeDtypeStruct((N_EXPERTS, b * cap, PACK_TILES, LANES), U32),
        compiler_params=_cparams(("parallel", "arbitrary")), name="moe_gather")(
            ids_g, h_pk.reshape(b, seq, PACK_TILES, LANES))

    assert (b * cap) % FFN_ROWS == 0 and FFN_ROWS % cap == 0
    slot_rows = cap * ROW_TILES
    seqs_per_step = FFN_ROWS // cap
    n_tiles = b * cap // FFN_ROWS
    chunk = D_FF // n_tiles
    assert chunk * n_tiles == D_FF and chunk % LANES == 0

    def cur(p):
        return jnp.maximum(p - 1, 0)

    def nxt(p):
        return jnp.minimum(p, N_EXPERTS - 1)

    def tile(p, i):
        return jnp.where(p > 0, i, 0)

    ye_rt = pl.pallas_call(
        functools.partial(_ffn_kernel, n_chunks=n_tiles), grid=(N_EXPERTS + 1, n_tiles),
        in_specs=[pl.BlockSpec((1, FFN_ROWS * PACK_TILES, LANES), lambda p, i: (cur(p), tile(p, i), 0)),
                  pl.BlockSpec((seqs_per_step, 1, cap, LANES), lambda p, i: (tile(p, i), cur(p), 0, 0)),
                  pl.BlockSpec((1, 1, D_MODEL, chunk), lambda p, i: (layer, nxt(p), 0, i)),
                  pl.BlockSpec((1, 1, D_MODEL, chunk), lambda p, i: (layer, nxt(p), 0, i)),
                  pl.BlockSpec((1, 1, chunk, D_MODEL), lambda p, i: (layer, nxt(p), i, 0))],
        out_specs=pl.BlockSpec((1, FFN_ROWS * ROW_TILES, LANES), lambda p, i: (cur(p), tile(p, i), 0)),
        out_shape=jax.ShapeDtypeStruct((N_EXPERTS, b * slot_rows, LANES), F32),
        scratch_shapes=[pltpu.VMEM((FFN_ROWS, D_MODEL), BF16),
                        pltpu.VMEM((2, D_MODEL, D_FF), BF16), pltpu.VMEM((2, D_MODEL, D_FF), BF16),
                        pltpu.VMEM((2, D_FF, D_MODEL), BF16)],
        compiler_params=_cparams(("arbitrary", "arbitrary")), name="moe_ffn")(
            xe_pk.reshape(N_EXPERTS, b * cap * PACK_TILES, LANES), gates, wg, wu, wd)

    assert seq % MERGE_ROWS == 0
    n_merge = seq // MERGE_ROWS

    def eg_of(st):
        return jnp.minimum(st, n_eg - 1)

    def rows_of(bi, st):
        return (bi * n_merge + jnp.maximum(st - n_eg, 0), 0)

    smem_step = pl.BlockSpec((1, 1, step_ids), lambda bi, st: (bi * n_eg + eg_of(st), 0, 0),
                             memory_space=pltpu.SMEM)
    return pl.pallas_call(
        functools.partial(_scatter_kernel, cap=cap, n_eg=n_eg), grid=(b, n_eg + n_merge),
        in_specs=[smem_step,
                  pl.BlockSpec((MOE_EXPERTS_PER_STEP, slot_rows, LANES), lambda bi, st: (eg_of(st), bi, 0)),
                  pl.BlockSpec((MERGE_ROWS, D_MODEL), rows_of)],
        out_specs=pl.BlockSpec((MERGE_ROWS, D_MODEL), rows_of),
        out_shape=jax.ShapeDtypeStruct((b * seq, D_MODEL), F32),
        scratch_shapes=[pltpu.VMEM((seq * ROW_TILES, LANES), F32)],
        compiler_params=_cparams(("parallel", "arbitrary")), name="moe_scatter_merge")(
            ids_g, ye_rt, x2d)


def _rope_tables(seq):
    t = jnp.arange(seq).astype(F32)
    inv = jnp.power(jnp.float32(ROPE_THETA), -jnp.arange(0, HEAD_DIM, 2, dtype=F32) / HEAD_DIM)
    ang = t[:, None] * inv[None, :]
    cos = jnp.concatenate([jnp.cos(ang), jnp.cos(ang)], axis=1)
    sin = jnp.concatenate([-jnp.sin(ang), jnp.sin(ang)], axis=1)
    return jnp.tile(cos, (1, 4)), jnp.tile(sin, (1, 4))


def _axial_tables(seq):
    t = jnp.arange(seq)
    half = HEAD_DIM // 2
    inv = jnp.power(jnp.float32(ROPE_THETA), -jnp.arange(0, half, 2, dtype=F32) / half)
    ang_r = (t // GRID_W).astype(F32)[:, None] * inv[None, :]
    ang_c = (t % GRID_W).astype(F32)[:, None] * inv[None, :]
    cos = jnp.concatenate([jnp.cos(ang_r), jnp.cos(ang_r), jnp.cos(ang_c), jnp.cos(ang_c)], axis=1)
    sin = jnp.concatenate([-jnp.sin(ang_r), jnp.sin(ang_r), -jnp.sin(ang_c), jnp.sin(ang_c)], axis=1)
    return jnp.tile(cos, (1, 4)), jnp.tile(sin, (1, 4))


def _col_gain(pieces):
    cols = []
    for gain, n_heads, scale in pieces:
        if gain is None:
            cols.append(jnp.ones((n_heads * HEAD_DIM,), F32))
        else:
            cols.append(jnp.tile(gain.astype(F32) * scale, n_heads))
    return jnp.concatenate(cols)[None, :]


def _chunks(start, width, kind, oi):
    out = []
    step = 256 if width % 256 == 0 else LANES
    for c in range(0, width, step):
        out.append((start + c, step, kind, oi, c))
    return out


def _moe_block(x2d, parts, w_out, ffn_gain, w_router, wg, wu, wd, layer, batch, seq):
    cap = EC_CAPACITY_FACTOR * seq // N_EXPERTS
    x_new, h_pk, aff, aff_t = _out_proj(x2d, parts, w_out.astype(BF16), ffn_gain, w_router, batch, seq)
    ids, gates = _topk(aff_t, aff.reshape(batch, seq, LANES), cap)
    return _moe(x_new, ids, gates, h_pk, wg, wu, wd, layer, cap)


def kernel(x, attn_norm_even, w_in_even, q_norm_a, k_norm_a, rel_bias_a, q_norm_b, k_norm_b, sink_b,
           w_out_even, attn_norm_odd, w_in_odd, q_norm_c, k_norm_c, w_out_odd,
           ffn_norm, w_router, w_gate, w_up, w_down):
    batch, seq, _ = x.shape
    m = batch * seq
    scale = HEAD_DIM ** -0.5 * LOG2E
    x2d = x.reshape(m, D_MODEL)
    depth = ffn_norm.shape[0]
    for layer in range(depth):
        i = layer // 2
        if layer % 2 == 0:
            cos_t, sin_t = _rope_tables(seq)
            colgain = _col_gain([(q_norm_a[i], A_HEADS, scale), (k_norm_a[i], A_HEADS, 1.0),
                                 (None, A_HEADS, 1.0), (q_norm_b[i], B_HEADS, scale),
                                 (k_norm_b[i], B_KV_HEADS, 1.0), (None, B_KV_HEADS, 1.0)])
            sections = (_chunks(0, A_W, "norm", 0) + _chunks(A_W, A_W, "norm", 1)
                        + _chunks(2 * A_W, A_W, "plain", 2) + _chunks(3 * A_W, B_QW, "rope", 3)
                        + _chunks(3 * A_W + B_QW, B_KVW, "rope", 4)
                        + _chunks(3 * A_W + B_QW + B_KVW, B_KVW, "plain", 5))
            widths = [A_W, A_W, A_W, B_QW, B_KVW, B_KVW]
            outs = _norm_proj(x2d, attn_norm_even[i], w_in_even[i].astype(BF16), colgain,
                              cos_t, sin_t, sections, widths, HEAD_DIM // 2, seq)
            qa, ka, va, qb, kb, vb = [o.reshape(batch, seq, -1) for o in outs]
            out_a = _na_attention(qa, ka, va, rel_bias_a[i])
            out_b = _swa_attention(qb, kb, vb, sink_b[i])
            parts = [out_a.reshape(m, A_W), out_b.reshape(m, B_QW)]
            w_out = w_out_even[i]
        else:
            cos_t, sin_t = _axial_tables(seq)
            colgain = _col_gain([(q_norm_c[i], C_HEADS, scale), (k_norm_c[i], C_KV_HEADS, 1.0),
                                 (None, C_KV_HEADS, 1.0)])
            sections = (_chunks(0, C_QW, "rope", 0) + _chunks(C_QW, C_KVW, "rope", 1)
                        + _chunks(C_QW + C_KVW, C_KVW, "plain", 2))
            widths = [C_QW, C_KVW, C_KVW]
            outs = _norm_proj(x2d, attn_norm_odd[i], w_in_odd[i].astype(BF16), colgain,
                              cos_t, sin_t, sections, widths, HEAD_DIM // 4, seq)
            qc, kc, vc = [o.reshape(batch, seq, -1) for o in outs]
            parts = [_dense_attention(qc, kc, vc).reshape(m, C_QW)]
            w_out = w_out_odd[i]
        x2d = _moe_block(x2d, parts, w_out, ffn_norm[layer], w_router[layer],
                         w_gate, w_up, w_down, layer, batch, seq)
    return x2d.reshape(batch, seq, D_MODEL)
```

```python
import functools

import numpy as np
import jax
import jax.numpy as jnp
from jax import lax
from jax.experimental import pallas as pl
from jax.experimental.pallas import tpu as pltpu

D_MODEL = 1024
HEAD_DIM = 64
GRID_W = 64
ROPE_THETA = 10000.0
EPS = 1e-6
NEG_INF = -1e30
LOG2E = 1.4426950408889634

A_HEADS = 8
NA_WIN_ROWS = 8
NA_WIN_COLS = 16
B_HEADS = 8
B_KV_HEADS = 2
B_WINDOW = 128
C_HEADS = 16
C_KV_HEADS = 4
N_EXPERTS = 16
EC_CAPACITY_FACTOR = 2
D_FF = 2048

A_W = A_HEADS * HEAD_DIM
B_QW = B_HEADS * HEAD_DIM
B_KVW = B_KV_HEADS * HEAD_DIM
C_QW = C_HEADS * HEAD_DIM
C_KVW = C_KV_HEADS * HEAD_DIM

SUBLANES = 8
LANES = 128
ROW_TILES = D_MODEL // LANES
VMEM_LIMIT = 56 * 1024 * 1024

TM = 512
NA_QROWS = 4
NA_KROWS = 12
SWA_Q = 256
SWA_K = 512
DENSE_Q = 256
DENSE_UNROLL = 4
DENSE_KEY_CHUNKS = 2

F32 = jnp.float32
BF16 = jnp.bfloat16


def _cparams(sem):
    return pltpu.CompilerParams(dimension_semantics=sem, vmem_limit_bytes=VMEM_LIMIT)


def _rms_rows(x):
    return x * lax.rsqrt(jnp.mean(x * x, axis=-1, keepdims=True) + EPS)


PACK_TILES = ROW_TILES // 2
HALF_D = D_MODEL // 2
U32 = jnp.uint32


def _pack_pair(lo, hi):
    lo_bits = pltpu.bitcast(lo.astype(BF16).astype(F32), U32) >> 16
    hi_bits = pltpu.bitcast(hi.astype(BF16).astype(F32), U32) & jnp.uint32(0xFFFF0000)
    return lo_bits | hi_bits


def _unpack_pair(words):
    lo = pltpu.bitcast(words << 16, F32)
    hi = pltpu.bitcast(words & jnp.uint32(0xFFFF0000), F32)
    return lo, hi


def _to_packed(src_ref, dst_ref, rows):
    def body(i, carry):
        r0 = pl.multiple_of(i * SUBLANES, SUBLANES)
        base = i * (SUBLANES * PACK_TILES)
        for c in range(PACK_TILES):
            lo = src_ref[pl.ds(r0, SUBLANES), c * LANES:(c + 1) * LANES]
            hi = src_ref[pl.ds(r0, SUBLANES), HALF_D + c * LANES:HALF_D + (c + 1) * LANES]
            dst_ref[pl.ds(base + c, SUBLANES, stride=PACK_TILES), :] = _pack_pair(lo, hi)
        return carry
    lax.fori_loop(0, rows // SUBLANES, body, 0)


def _norm_proj_kernel(x_ref, gain_ref, w_ref, cg_ref, cos_ref, sin_ref, bd_ref, *outs, sections, half):
    hb = (_rms_rows(x_ref[...]) * gain_ref[...]).astype(BF16)

    ys = [jnp.dot(hb, w_ref[:, c0:c0 + width], preferred_element_type=F32)
          for (c0, width, _, _, _) in sections]
    sss = []
    for y, (c0, width, kind, _, _) in zip(ys, sections):
        if kind == "plain":
            sss.append(None)
            continue
        y2 = y * y
        hi = y2.astype(BF16)
        lo = (y2 - hi.astype(F32)).astype(BF16)
        bd = bd_ref[:width, :width]
        sss.append(jnp.dot(hi, bd, preferred_element_type=F32)
                   + jnp.dot(lo, bd, preferred_element_type=F32))
    for y, ss, (c0, width, kind, oi, oc) in zip(ys, sss, sections):
        if kind != "plain":
            y = y * lax.rsqrt(ss + EPS) * cg_ref[:, c0:c0 + width]
        if kind == "rope":
            lane = lax.broadcasted_iota(jnp.int32, (1, width), 1)
            first = (lane % (2 * half)) < half
            part = jnp.where(first, pltpu.roll(y, width - half, 1), pltpu.roll(y, half, 1))
            y = y * cos_ref[:, :width] + part * sin_ref[:, :width]
        outs[oi][:, oc:oc + width] = y.astype(BF16)


def _norm_proj(x2d, gain, w_bf16, colgain, cos_t, sin_t, sections, out_widths, half, seq):
    m = x2d.shape[0]
    n_in = w_bf16.shape[1]
    bd = jnp.asarray(np.kron(np.eye(4), np.full((HEAD_DIM, HEAD_DIM), 1.0 / HEAD_DIM)), BF16)
    tiles_per_seq = seq // TM
    in_specs = [
        pl.BlockSpec((TM, D_MODEL), lambda i: (i, 0)),
        pl.BlockSpec((1, D_MODEL), lambda i: (0, 0)),
        pl.BlockSpec((D_MODEL, n_in), lambda i: (0, 0)),
        pl.BlockSpec((1, n_in), lambda i: (0, 0)),
        pl.BlockSpec((TM, 256), lambda i: (i % tiles_per_seq, 0)),
        pl.BlockSpec((TM, 256), lambda i: (i % tiles_per_seq, 0)),
        pl.BlockSpec((256, 256), lambda i: (0, 0)),
    ]
    out_shape = [jax.ShapeDtypeStruct((m, wd), BF16) for wd in out_widths]
    out_specs = [pl.BlockSpec((TM, wd), lambda i: (i, 0)) for wd in out_widths]
    kern = functools.partial(_norm_proj_kernel, sections=tuple(sections), half=half)
    return pl.pallas_call(
        kern, grid=(m // TM,), in_specs=in_specs, out_specs=out_specs, out_shape=out_shape,
        compiler_params=_cparams(("parallel",)), name="norm_proj")(
            x2d, gain.reshape(1, D_MODEL), w_bf16, colgain, cos_t, sin_t, bd)


def _na_kernel(q_ref, k_ref, v_ref, bias_ref, o_ref):
    nq = NA_QROWS * GRID_W
    nk = NA_KROWS * GRID_W
    rows = q_ref.shape[1] // GRID_W
    n_blocks = rows // NA_QROWS
    lane = lax.broadcasted_iota(jnp.int32, (1, LANES), 1)
    head0 = lane < HEAD_DIM

    edge_keys = NA_WIN_ROWS * GRID_W
    for qb in range(n_blocks):
        r0 = qb * NA_QROWS
        start = min(max(r0 - NA_WIN_ROWS // 2, 0), rows - NA_KROWS)
        pat = 0 if qb == 0 else (2 if qb == n_blocks - 1 else 1)
        c0, c1 = {0: (0, edge_keys), 1: (0, nk), 2: (nk - edge_keys, nk)}[pat]
        qs = r0 * GRID_W
        ks = start * GRID_W + c0
        q = q_ref[0, qs:qs + nq, :]
        k = k_ref[0, ks:ks + (c1 - c0), :]
        v = v_ref[0, ks:ks + (c1 - c0), :]
        acc = None
        for hh in range(2):
            sel = head0 if hh == 0 else jnp.logical_not(head0)
            qm = jnp.where(sel, q, jnp.zeros_like(q))
            s = lax.dot_general(qm, k, (((1,), (1,)), ((), ())), preferred_element_type=F32)
            s = s + bias_ref[pat, hh, :, c0:c1]
            m = jnp.max(s, axis=-1, keepdims=True)
            p = jnp.exp2(s - m)
            l = jnp.sum(p, axis=-1, keepdims=True)
            o = jnp.dot(p.astype(BF16), v, preferred_element_type=F32) / l
            acc = o if acc is None else jnp.where(sel, o, acc)
        o_ref[0, qs:qs + nq, :] = acc.astype(BF16)


def _na_bias_tables(rel_bias, rows):
    n_dr, n_dc = 2 * NA_WIN_ROWS - 1, 2 * NA_WIN_COLS - 1
    sel_r = np.zeros((3, NA_QROWS, NA_KROWS, n_dr), np.float32)
    in_r = np.zeros((3, NA_QROWS, NA_KROWS), np.float32)
    for p, r0 in enumerate((0, 2 * NA_QROWS, rows - NA_QROWS)):
        start = int(np.clip(r0 - NA_WIN_ROWS // 2, 0, rows - NA_KROWS))
        for lr in range(NA_QROWS):
            r = r0 + lr
            rs = int(np.clip(r - NA_WIN_ROWS // 2, 0, rows - NA_WIN_ROWS))
            for kl in range(NA_KROWS):
                kr = start + kl
                if rs <= kr < rs + NA_WIN_ROWS:
                    in_r[p, lr, kl] = 1.0
                    sel_r[p, lr, kl, kr - r + NA_WIN_ROWS - 1] = 1.0
    sel_c = np.zeros((GRID_W, GRID_W, n_dc), np.float32)
    in_c = np.zeros((GRID_W, GRID_W), np.float32)
    for c in range(GRID_W):
        cs = int(np.clip(c - NA_WIN_COLS // 2, 0, GRID_W - NA_WIN_COLS))
        for kc in range(cs, cs + NA_WIN_COLS):
            in_c[c, kc] = 1.0
            dc = int(np.clip(kc - c, -(NA_WIN_COLS - 1), NA_WIN_COLS - 1))
            sel_c[c, kc, dc + NA_WIN_COLS - 1] = 1.0
    hp = lax.Precision.HIGHEST
    rows_sel = jnp.einsum('plkd,hde->phlke', jnp.asarray(sel_r), rel_bias.astype(F32), precision=hp)
    vals = jnp.einsum('phlke,cje->phlckj', rows_sel, jnp.asarray(sel_c), precision=hp)
    inside = jnp.asarray(in_r[:, None, :, None, :, None] * in_c[None, None, None, :, None, :])
    tab = jnp.where(inside > 0.5, vals * LOG2E, NEG_INF)
    return tab.reshape(3, rel_bias.shape[0], NA_QROWS * GRID_W, NA_KROWS * GRID_W)


def _na_attention(q, k, v, rel_bias):
    b, s, _ = q.shape
    rows = s // GRID_W
    assert rows % NA_QROWS == 0 and rows >= NA_KROWS and rows // NA_QROWS >= 3
    assert NA_QROWS <= NA_WIN_ROWS // 2
    nq, nk = NA_QROWS * GRID_W, NA_KROWS * GRID_W
    bias = _na_bias_tables(rel_bias, rows)
    blk = pl.BlockSpec((1, s, LANES), lambda bi, hp: (bi, 0, hp))
    return pl.pallas_call(
        _na_kernel, grid=(b, A_HEADS // 2),
        in_specs=[blk, blk, blk, pl.BlockSpec((3, 2, nq, nk), lambda bi, hp: (0, hp, 0, 0))],
        out_specs=blk, out_shape=jax.ShapeDtypeStruct((b, s, A_W), BF16),
        compiler_params=_cparams(("parallel", "arbitrary")), name="na_attention")(q, k, v, bias)


def _swa_kernel(sink_ref, q_ref, k_ref, v_ref, rep_ref, o_ref, kt_ref, vt_ref):
    s_len = q_ref.shape[1]
    n_groups = B_HEADS // B_KV_HEADS
    gw = n_groups * HEAD_DIM
    lane = lax.broadcasted_iota(jnp.int32, (1, gw), 1)
    rel = (lax.broadcasted_iota(jnp.int32, (SWA_Q, SWA_K), 0)
           - lax.broadcasted_iota(jnp.int32, (SWA_Q, SWA_K), 1))

    for g in range(B_KV_HEADS):
        kt_ref[...] = jnp.dot(k_ref[0], rep_ref[g], preferred_element_type=F32).astype(BF16)
        vt_ref[...] = jnp.dot(v_ref[0], rep_ref[g], preferred_element_type=F32).astype(BF16)

        for i in range(s_len // SWA_Q):
            q0 = i * SWA_Q
            k0 = min(max(q0 - B_WINDOW, 0), s_len - SWA_K)
            q = q_ref[0, q0:q0 + SWA_Q, g * gw:(g + 1) * gw]
            kt = kt_ref[k0:k0 + SWA_K, :]
            vt = vt_ref[k0:k0 + SWA_K, :]
            valid = jnp.abs(rel + (q0 - k0)) <= B_WINDOW
            acc = jnp.zeros((SWA_Q, gw), F32)
            for j in range(n_groups):
                sel = (lane >= j * HEAD_DIM) & (lane < (j + 1) * HEAD_DIM)
                sink = sink_ref[g * n_groups + j]
                qm = jnp.where(sel, q, jnp.zeros_like(q))
                s = lax.dot_general(qm, kt, (((1,), (1,)), ((), ())), preferred_element_type=F32)
                s = jnp.where(valid, s, NEG_INF)
                m = jnp.maximum(jnp.max(s, axis=-1, keepdims=True), sink)
                e = jnp.exp2(s - m)
                den = jnp.sum(e, axis=-1, keepdims=True) + jnp.exp2(sink - m)
                o = jnp.dot(e.astype(BF16), vt, preferred_element_type=F32) / den
                acc = jnp.where(sel, o, acc)
            o_ref[0, q0:q0 + SWA_Q, g * gw:(g + 1) * gw] = acc.astype(BF16)


def _replication_matrices(n_kv):
    rep = np.zeros((n_kv, n_kv * HEAD_DIM, 4 * HEAD_DIM), np.float32)
    for g in range(n_kv):
        for j in range(4 * HEAD_DIM):
            rep[g, g * HEAD_DIM + j % HEAD_DIM, j] = 1.0
    return jnp.asarray(rep, BF16)


def _swa_attention(q, k, v, sink):
    b, s, _ = q.shape
    assert s % SWA_Q == 0 and s >= SWA_K and SWA_K >= SWA_Q + 2 * B_WINDOW
    rep = _replication_matrices(B_KV_HEADS)
    return pl.pallas_call(
        _swa_kernel, grid=(b,),
        in_specs=[pl.BlockSpec(memory_space=pltpu.SMEM),
                  pl.BlockSpec((1, s, B_QW), lambda bi: (bi, 0, 0)),
                  pl.BlockSpec((1, s, B_KVW), lambda bi: (bi, 0, 0)),
                  pl.BlockSpec((1, s, B_KVW), lambda bi: (bi, 0, 0)),
                  pl.BlockSpec((B_KV_HEADS, B_KVW, 4 * HEAD_DIM), lambda bi: (0, 0, 0))],
        out_specs=pl.BlockSpec((1, s, B_QW), lambda bi: (bi, 0, 0)),
        out_shape=jax.ShapeDtypeStruct((b, s, B_QW), BF16),
        scratch_shapes=[pltpu.VMEM((s, 4 * HEAD_DIM), BF16), pltpu.VMEM((s, 4 * HEAD_DIM), BF16)],
        compiler_params=_cparams(("parallel",)), name="swa_attention")(sink.astype(F32) * LOG2E, q, k, v, rep)


def _dense_kernel(q_ref, k_ref, v_ref, rep_ref, o_ref, kt_ref, vt_ref):
    s_len = q_ref.shape[1]
    gw = 4 * HEAD_DIM
    lane = lax.broadcasted_iota(jnp.int32, (1, gw), 1)
    kt_ref[...] = jnp.dot(k_ref[0], rep_ref[0], preferred_element_type=F32).astype(BF16)
    vt_ref[...] = jnp.dot(v_ref[0], rep_ref[0], preferred_element_type=F32).astype(BF16)

    def body(i, carry):
        for u in range(DENSE_UNROLL):
            q0 = pl.multiple_of((i * DENSE_UNROLL + u) * DENSE_Q, DENSE_Q)
            q = q_ref[0, pl.ds(q0, DENSE_Q), :]
            acc = jnp.zeros((DENSE_Q, gw), F32)
            for j in range(4):
                sel = (lane >= j * HEAD_DIM) & (lane < (j + 1) * HEAD_DIM)
                qm = jnp.where(sel, q, jnp.zeros_like(q))
                kc = s_len // DENSE_KEY_CHUNKS
                m = l = pv = None
                for c in range(DENSE_KEY_CHUNKS):
                    s = lax.dot_general(qm, kt_ref[c * kc:(c + 1) * kc, :], (((1,), (1,)), ((), ())),
                                        preferred_element_type=F32)
                    mc = jnp.max(s, axis=-1, keepdims=True)
                    m_new = mc if m is None else jnp.maximum(m, mc)
                    p = jnp.exp2(s - m_new)
                    part = jnp.dot(p.astype(BF16), vt_ref[c * kc:(c + 1) * kc, :], preferred_element_type=F32)
                    if m is None:
                        l, pv = jnp.sum(p, axis=-1, keepdims=True), part
                    else:
                        alpha = jnp.exp2(m - m_new)
                        l = alpha * l + jnp.sum(p, axis=-1, keepdims=True)
                        pv = alpha * pv + part
                    m = m_new
                acc = jnp.where(sel, pv / l, acc)
            o_ref[0, pl.ds(q0, DENSE_Q), :] = acc.astype(BF16)
        return carry

    lax.fori_loop(0, s_len // (DENSE_Q * DENSE_UNROLL), body, 0)


def _dense_attention(q, k, v):
    b, s, _ = q.shape
    gw = 4 * HEAD_DIM
    rep = _replication_matrices(C_KV_HEADS)
    return pl.pallas_call(
        _dense_kernel, grid=(b, C_KV_HEADS),
        in_specs=[pl.BlockSpec((1, s, gw), lambda bi, g: (bi, 0, g)),
                  pl.BlockSpec((1, s, C_KVW), lambda bi, g: (bi, 0, 0)),
                  pl.BlockSpec((1, s, C_KVW), lambda bi, g: (bi, 0, 0)),
                  pl.BlockSpec((1, C_KVW, gw), lambda bi, g: (g, 0, 0))],
        out_specs=pl.BlockSpec((1, s, gw), lambda bi, g: (bi, 0, g)),
        out_shape=jax.ShapeDtypeStruct((b, s, C_QW), BF16),
        scratch_shapes=[pltpu.VMEM((s, gw), BF16), pltpu.VMEM((s, gw), BF16)],
        compiler_params=_cparams(("parallel", "arbitrary")), name="dense_attention")(q, k, v, rep)


def _out_proj_kernel(*refs, n_parts):
    it = iter(refs)
    x_ref = next(it)
    parts = [next(it) for _ in range(n_parts)]
    w_ref = next(it)
    gain_ref = next(it)
    wr_ref = next(it)
    xo_ref = next(it)
    hpk_ref = next(it)
    aff_ref = next(it)
    afft_ref = next(it)
    h_ref = next(it)

    y = x_ref[...]
    c0 = 0
    for p_ref in parts:
        wp = p_ref.shape[1]
        y = y + jnp.dot(p_ref[...], w_ref[c0:c0 + wp, :], preferred_element_type=F32)
        c0 += wp
    xo_ref[...] = y
    h = _rms_rows(y) * gain_ref[...]
    h_ref[...] = h
    _to_packed(h_ref, hpk_ref, TM)

    h_hi = h.astype(BF16)
    h_lo = (h - h_hi.astype(F32)).astype(BF16)
    r_hi = jnp.dot(h_hi, wr_ref[...], preferred_element_type=F32)
    r_lo = jnp.dot(h_lo, wr_ref[:, :LANES], preferred_element_type=F32)
    lane = lax.broadcasted_iota(jnp.int32, (1, LANES), 1)
    logits = r_hi[:, :LANES] + r_hi[:, LANES:] + r_lo
    logits = jnp.where(lane < N_EXPERTS, logits, NEG_INF)
    e = jnp.exp(logits - jnp.max(logits, axis=-1, keepdims=True))
    aff = e / jnp.sum(e, axis=-1, keepdims=True)
    aff_ref[...] = aff
    afft_ref[0] = jnp.transpose(aff)[:N_EXPERTS, :]


def _out_proj(x2d, parts, w_out_bf16, ffn_gain, w_router, batch, seq):
    m = x2d.shape[0]
    tiles_per_seq = seq // TM
    wr = w_router.astype(F32)
    wr_hi = wr.astype(BF16)
    wr_lo = (wr - wr_hi.astype(F32)).astype(BF16)
    wr_split = jnp.zeros((D_MODEL, 2 * LANES), BF16)
    wr_split = wr_split.at[:, :N_EXPERTS].set(wr_hi).at[:, LANES:LANES + N_EXPERTS].set(wr_lo)
    in_specs = [pl.BlockSpec((TM, D_MODEL), lambda i: (i, 0))]
    in_specs += [pl.BlockSpec((TM, p.shape[1]), lambda i: (i, 0)) for p in parts]
    in_specs += [pl.BlockSpec((D_MODEL, D_MODEL), lambda i: (0, 0)),
                 pl.BlockSpec((1, D_MODEL), lambda i: (0, 0)),
                 pl.BlockSpec((D_MODEL, 2 * LANES), lambda i: (0, 0))]
    out_shape = [jax.ShapeDtypeStruct((m, D_MODEL), F32),
                 jax.ShapeDtypeStruct((m * PACK_TILES, LANES), U32),
                 jax.ShapeDtypeStruct((m, LANES), F32),
                 jax.ShapeDtypeStruct((batch, N_EXPERTS, seq), F32)]
    out_specs = [pl.BlockSpec((TM, D_MODEL), lambda i: (i, 0)),
                 pl.BlockSpec((TM * PACK_TILES, LANES), lambda i: (i, 0)),
                 pl.BlockSpec((TM, LANES), lambda i: (i, 0)),
                 pl.BlockSpec((1, N_EXPERTS, TM), lambda i: (i // tiles_per_seq, 0, i % tiles_per_seq))]
    kern = functools.partial(_out_proj_kernel, n_parts=len(parts))
    return pl.pallas_call(
        kern, grid=(m // TM,), in_specs=in_specs, out_specs=out_specs, out_shape=out_shape,
        scratch_shapes=[pltpu.VMEM((TM, D_MODEL), F32)],
        compiler_params=_cparams(("parallel",)), name="out_proj_router")(
            x2d, *parts, w_out_bf16, ffn_gain.reshape(1, D_MODEL), wr_split)


def _topk_kernel(afft_ref, aff_ref, ids_ref, gate_ref, res_ref, *, cap):
    s_len = afft_ref.shape[2]
    at = afft_ref[0]
    bits = pltpu.bitcast(at, jnp.int32)

    def count(mask):
        return jnp.sum(jnp.where(mask, 1.0, 0.0), axis=1, keepdims=True)

    def bit_step(k, thr):
        cand = thr | jnp.left_shift(jnp.int32(1), 30 - k)
        return jnp.where(count(bits >= cand) >= cap, cand, thr)

    thr = lax.fori_loop(0, 31, bit_step, jnp.zeros((N_EXPERTS, 1), jnp.int32))
    gt = bits > thr
    eq = bits == thr
    need = cap - count(gt)

    ri = lax.broadcasted_iota(jnp.int32, (LANES, LANES), 0)
    ci = lax.broadcasted_iota(jnp.int32, (LANES, LANES), 1)
    upper = jnp.where(ri < ci, 1.0, 0.0).astype(BF16)

    def prefix(mask_f32):
        out = []
        run = jnp.zeros((N_EXPERTS, 1), F32)
        for c in range(s_len // LANES):
            blk = mask_f32[:, c * LANES:(c + 1) * LANES]
            out.append(jnp.dot(blk.astype(BF16), upper, preferred_element_type=F32) + run)
            run = run + jnp.sum(blk, axis=1, keepdims=True)
        return out

    eq_f = jnp.where(eq, 1.0, 0.0)
    eq_rank = prefix(eq_f)
    sel_tiles = []
    for c in range(s_len // LANES):
        sl = slice(c * LANES, (c + 1) * LANES)
        sel_tiles.append(jnp.where(gt[:, sl] | (eq[:, sl] & (eq_rank[c] < need)), 1.0, 0.0))
    sel_f = jnp.concatenate(sel_tiles, axis=1)
    pos = jnp.concatenate(prefix(sel_f), axis=1)
    key = jnp.where(sel_f > 0.5, pos, -1.0)

    a = aff_ref[0]
    a_hi = a.astype(BF16).astype(F32)
    r1 = a - a_hi
    a_mid = r1.astype(BF16).astype(F32)
    a_lo = r1 - a_mid
    lane = lax.broadcasted_iota(jnp.int32, (s_len, LANES), 1)
    tok = lax.broadcasted_iota(jnp.int32, (s_len, LANES), 0)
    packed = a_hi + pltpu.roll(a_mid, N_EXPERTS, 1) + pltpu.roll(a_lo, 2 * N_EXPERTS, 1)
    packed = jnp.where(lane == 64, (tok // 64).astype(F32), packed)
    packed = jnp.where(lane == 65, (tok % 64).astype(F32), packed)
    rmat = packed.astype(BF16)

    slot = lax.broadcasted_iota(jnp.int32, (cap, 1), 0).astype(F32)
    for e in range(N_EXPERTS):
        onehot_t = jnp.where(key[e:e + 1, :] == slot, 1.0, 0.0).astype(BF16)
        res_ref[...] = jnp.dot(onehot_t, rmat, preferred_element_type=F32)
        res = res_ref[...]
        res_t = jnp.transpose(res)
        ids_ref[0, e:e + 1, :] = (res_t[64:65, :] * 64.0 + res_t[65:66, :]).astype(jnp.int32)
        gate = (res[:, e:e + 1] + res[:, N_EXPERTS + e:N_EXPERTS + e + 1]
                + res[:, 2 * N_EXPERTS + e:2 * N_EXPERTS + e + 1])
        gate_ref[0, e] = jnp.broadcast_to(gate, (cap, LANES))


def _topk(aff_t, aff, cap):
    b, _, s = aff_t.shape
    kern = functools.partial(_topk_kernel, cap=cap)
    return pl.pallas_call(
        kern, grid=(b,),
        in_specs=[pl.BlockSpec((1, N_EXPERTS, s), lambda bi: (bi, 0, 0)),
                  pl.BlockSpec((1, s, LANES), lambda bi: (bi, 0, 0))],
        out_specs=[pl.BlockSpec((1, N_EXPERTS, cap), lambda bi: (bi, 0, 0)),
                   pl.BlockSpec((1, N_EXPERTS, cap, LANES), lambda bi: (bi, 0, 0, 0))],
        out_shape=[jax.ShapeDtypeStruct((b, N_EXPERTS, cap), jnp.int32),
                   jax.ShapeDtypeStruct((b, N_EXPERTS, cap, LANES), F32)],
        scratch_shapes=[pltpu.VMEM((cap, LANES), F32)],
        compiler_params=_cparams(("parallel",)), name="expert_topk")(aff_t, aff)


MOE_GROUP = 16
SCATTER_GROUP = 8
MERGE_ROWS = 1024


MOE_EXPERTS_PER_STEP = 4
FFN_ROWS = 512
FFN_F_CHUNK = 512


def _gather_kernel(ids_ref, h_ref, o_ref, *, cap):
    for el in range(MOE_EXPERTS_PER_STEP):
        def body(jg, carry, el=el):
            for u in range(MOE_GROUP):
                j = jg * MOE_GROUP + u
                o_ref[el, j] = h_ref[0, ids_ref[0, 0, el * cap + j]]
            return carry
        lax.fori_loop(0, cap // MOE_GROUP, body, 0)


def _ffn_kernel(xe_ref, gate_ref, wg_ref, wu_ref, wd_ref, o_ref, xb_ref, wgb_ref, wub_ref, wdb_ref,
                *, n_chunks):
    phase = pl.program_id(0)
    step = pl.program_id(1)
    n_experts = pl.num_programs(0) - 1
    load_slot = phase % 2
    chunk = D_FF // n_chunks
    for c in range(n_chunks):
        @pl.when((step == c) & (phase < n_experts))
        def _(c=c):
            cols = slice(c * chunk, (c + 1) * chunk)
            wgb_ref[load_slot, :, cols] = wg_ref[0, 0].astype(BF16)
            wub_ref[load_slot, :, cols] = wu_ref[0, 0].astype(BF16)
            wdb_ref[load_slot, cols, :] = wd_ref[0, 0].astype(BF16)

    @pl.when(phase > 0)
    def _():
        _ffn_step(xe_ref, gate_ref, wgb_ref, wub_ref, wdb_ref, o_ref, xb_ref, 1 - load_slot)


def _ffn_step(xe_ref, gate_ref, wg_ref, wu_ref, wd_ref, o_ref, xb_ref, slot):
    pair = 2 * SUBLANES
    for i in range(FFN_ROWS // pair):
        for c in range(PACK_TILES):
            base = i * pair * PACK_TILES + c
            lo_a, hi_a = _unpack_pair(xe_ref[0, pl.ds(base, SUBLANES, stride=PACK_TILES), :])
            lo_b, hi_b = _unpack_pair(
                xe_ref[0, pl.ds(base + SUBLANES * PACK_TILES, SUBLANES, stride=PACK_TILES), :])
            rows = slice(i * pair, (i + 1) * pair)
            xb_ref[rows, c * LANES:(c + 1) * LANES] = jnp.concatenate([lo_a, lo_b], axis=0).astype(BF16)
            xb_ref[rows, HALF_D + c * LANES:HALF_D + (c + 1) * LANES] = (
                jnp.concatenate([hi_a, hi_b], axis=0).astype(BF16))
    xe = xb_ref[...]
    y = None
    for f0 in range(0, D_FF, FFN_F_CHUNK):
        hg = jnp.dot(xe, wg_ref[slot, :, f0:f0 + FFN_F_CHUNK], preferred_element_type=F32)
        hu = jnp.dot(xe, wu_ref[slot, :, f0:f0 + FFN_F_CHUNK], preferred_element_type=F32)
        hid = (hg * (1.0 / (1.0 + jnp.exp(-hg))) * hu).astype(BF16)
        part = jnp.dot(hid, wd_ref[slot, f0:f0 + FFN_F_CHUNK, :], preferred_element_type=F32)
        y = part if y is None else y + part
    cap = gate_ref.shape[2]
    for i in range(FFN_ROWS // SUBLANES):
        r0 = i * SUBLANES
        gate = gate_ref[r0 // cap, 0, r0 % cap:r0 % cap + SUBLANES, :]
        for c in range(ROW_TILES):
            o_ref[0, pl.ds(r0 * ROW_TILES + c, SUBLANES, stride=ROW_TILES), :] = (
                y[r0:r0 + SUBLANES, c * LANES:(c + 1) * LANES] * gate)


def _scatter_kernel(ids_ref, y_ref, x_ref, o_ref, acc_ref, *, cap, n_eg):
    step = pl.program_id(1)

    @pl.when(step == 0)
    def _():
        acc_ref[...] = jnp.zeros_like(acc_ref)

    @pl.when(step < n_eg)
    def _():
        for el in range(MOE_EXPERTS_PER_STEP):
            def body(jg, carry, el=el):
                rows = []
                for u in range(SCATTER_GROUP):
                    j = jg * SCATTER_GROUP + u
                    rows.append(pl.multiple_of(ids_ref[0, 0, el * cap + j] * ROW_TILES, ROW_TILES))
                vals = []
                for u in range(SCATTER_GROUP):
                    j = jg * SCATTER_GROUP + u
                    src = pl.multiple_of(j * ROW_TILES, ROW_TILES)
                    vals.append(acc_ref[pl.ds(rows[u], ROW_TILES), :] + y_ref[el, pl.ds(src, ROW_TILES), :])
                for u in range(SCATTER_GROUP):
                    acc_ref[pl.ds(rows[u], ROW_TILES), :] = vals[u]
                return carry
            lax.fori_loop(0, cap // SCATTER_GROUP, body, 0)

    @pl.when(step >= n_eg)
    def _():
        first = (step - n_eg) * (MERGE_ROWS * ROW_TILES)

        def body(i, carry):
            r0 = pl.multiple_of(i * SUBLANES, SUBLANES)
            base = first + i * (SUBLANES * ROW_TILES)
            for c in range(ROW_TILES):
                cols = slice(c * LANES, (c + 1) * LANES)
                o_ref[pl.ds(r0, SUBLANES), cols] = (
                    x_ref[pl.ds(r0, SUBLANES), cols]
                    + acc_ref[pl.ds(base + c, SUBLANES, stride=ROW_TILES), :])
            return carry
        lax.fori_loop(0, MERGE_ROWS // SUBLANES, body, 0)


def _moe(x2d, ids, gates, h_pk, wg, wu, wd, layer, cap):
    b = ids.shape[0]
    seq = h_pk.shape[0] // (b * PACK_TILES)
    n_eg = N_EXPERTS // MOE_EXPERTS_PER_STEP
    step_ids = MOE_EXPERTS_PER_STEP * cap
    ids_g = ids.reshape(b * n_eg, 1, step_ids)
    smem_spec = pl.BlockSpec((1, 1, step_ids), lambda bi, eg: (bi * n_eg + eg, 0, 0), memory_space=pltpu.SMEM)

    xe_pk = pl.pallas_call(
        functools.partial(_gather_kernel, cap=cap), grid=(b, n_eg),
        in_specs=[smem_spec, pl.BlockSpec((1, seq, PACK_TILES, LANES), lambda bi, eg: (bi, 0, 0, 0))],
        out_specs=pl.BlockSpec((MOE_EXPERTS_PER_STEP, cap, PACK_TILES, LANES), lambda bi, eg: (eg, bi, 0, 0)),
        out_shape=jax.ShapeDtypeStruct((N_EXPERTS, b * cap, PACK_TILES, LANES), U32),
        compiler_params=_cparams(("parallel", "arbitrary")), name="moe_gather")(
            ids_g, h_pk.reshape(b, seq, PACK_TILES, LANES))

    assert (b * cap) % FFN_ROWS == 0 and FFN_ROWS % cap == 0
    slot_rows = cap * ROW_TILES
    seqs_per_step = FFN_ROWS // cap
    n_tiles = b * cap // FFN_ROWS
    chunk = D_FF // n_tiles
    assert chunk * n_tiles == D_FF and chunk % LANES == 0

    def cur(p):
        return jnp.maximum(p - 1, 0)

    def nxt(p):
        return jnp.minimum(p, N_EXPERTS - 1)

    def tile(p, i):
        return jnp.where(p > 0, i, 0)

    ye_rt = pl.pallas_call(
        functools.partial(_ffn_kernel, n_chunks=n_tiles), grid=(N_EXPERTS + 1, n_tiles),
        in_specs=[pl.BlockSpec((1, FFN_ROWS * PACK_TILES, LANES), lambda p, i: (cur(p), tile(p, i), 0)),
                  pl.BlockSpec((seqs_per_step, 1, cap, LANES), lambda p, i: (tile(p, i), cur(p), 0, 0)),
                  pl.BlockSpec((1, 1, D_MODEL, chunk), lambda p, i: (layer, nxt(p), 0, i)),
                  pl.BlockSpec((1, 1, D_MODEL, chunk), lambda p, i: (layer, nxt(p), 0, i)),
                  pl.BlockSpec((1, 1, chunk, D_MODEL), lambda p, i: (layer, nxt(p), i, 0))],
        out_specs=pl.BlockSpec((1, FFN_ROWS * ROW_TILES, LANES), lambda p, i: (cur(p), tile(p, i), 0)),
        out_shape=jax.ShapeDtypeStruct((N_EXPERTS, b * slot_rows, LANES), F32),
        scratch_shapes=[pltpu.VMEM((FFN_ROWS, D_MODEL), BF16),
                        pltpu.VMEM((2, D_MODEL, D_FF), BF16), pltpu.VMEM((2, D_MODEL, D_FF), BF16),
                        pltpu.VMEM((2, D_FF, D_MODEL), BF16)],
        compiler_params=_cparams(("arbitrary", "arbitrary")), name="moe_ffn")(
            xe_pk.reshape(N_EXPERTS, b * cap * PACK_TILES, LANES), gates, wg, wu, wd)

    assert seq % MERGE_ROWS == 0
    n_merge = seq // MERGE_ROWS

    def eg_of(st):
        return jnp.minimum(st, n_eg - 1)

    def rows_of(bi, st):
        return (bi * n_merge + jnp.maximum(st - n_eg, 0), 0)

    smem_step = pl.BlockSpec((1, 1, step_ids), lambda bi, st: (bi * n_eg + eg_of(st), 0, 0),
                             memory_space=pltpu.SMEM)
    return pl.pallas_call(
        functools.partial(_scatter_kernel, cap=cap, n_eg=n_eg), grid=(b, n_eg + n_merge),
        in_specs=[smem_step,
                  pl.BlockSpec((MOE_EXPERTS_PER_STEP, slot_rows, LANES), lambda bi, st: (eg_of(st), bi, 0)),
                  pl.BlockSpec((MERGE_ROWS, D_MODEL), rows_of)],
        out_specs=pl.BlockSpec((MERGE_ROWS, D_MODEL), rows_of),
        out_shape=jax.ShapeDtypeStruct((b * seq, D_MODEL), F32),
        scratch_shapes=[pltpu.VMEM((seq * ROW_TILES, LANES), F32)],
        compiler_params=_cparams(("parallel", "arbitrary")), name="moe_scatter_merge")(
            ids_g, ye_rt, x2d)


def _rope_tables(seq):
    t = jnp.arange(seq).astype(F32)
    inv = jnp.power(jnp.float32(ROPE_THETA), -jnp.arange(0, HEAD_DIM, 2, dtype=F32) / HEAD_DIM)
    ang = t[:, None] * inv[None, :]
    cos = jnp.concatenate([jnp.cos(ang), jnp.cos(ang)], axis=1)
    sin = jnp.concatenate([-jnp.sin(ang), jnp.sin(ang)], axis=1)
    return jnp.tile(cos, (1, 4)), jnp.tile(sin, (1, 4))


def _axial_tables(seq):
    t = jnp.arange(seq)
    half = HEAD_DIM // 2
    inv = jnp.power(jnp.float32(ROPE_THETA), -jnp.arange(0, half, 2, dtype=F32) / half)
    ang_r = (t // GRID_W).astype(F32)[:, None] * inv[None, :]
    ang_c = (t % GRID_W).astype(F32)[:, None] * inv[None, :]
    cos = jnp.concatenate([jnp.cos(ang_r), jnp.cos(ang_r), jnp.cos(ang_c), jnp.cos(ang_c)], axis=1)
    sin = jnp.concatenate([-jnp.sin(ang_r), jnp.sin(ang_r), -jnp.sin(ang_c), jnp.sin(ang_c)], axis=1)
    return jnp.tile(cos, (1, 4)), jnp.tile(sin, (1, 4))


def _col_gain(pieces):
    cols = []
    for gain, n_heads, scale in pieces:
        if gain is None:
            cols.append(jnp.ones((n_heads * HEAD_DIM,), F32))
        else:
            cols.append(jnp.tile(gain.astype(F32) * scale, n_heads))
    return jnp.concatenate(cols)[None, :]


def _chunks(start, width, kind, oi):
    out = []
    step = 256 if width % 256 == 0 else LANES
    for c in range(0, width, step):
        out.append((start + c, step, kind, oi, c))
    return out


def _moe_block(x2d, parts, w_out, ffn_gain, w_router, wg, wu, wd, layer, batch, seq):
    cap = EC_CAPACITY_FACTOR * seq // N_EXPERTS
    x_new, h_pk, aff, aff_t = _out_proj(x2d, parts, w_out.astype(BF16), ffn_gain, w_router, batch, seq)
    ids, gates = _topk(aff_t, aff.reshape(batch, seq, LANES), cap)
    return _moe(x_new, ids, gates, h_pk, wg, wu, wd, layer, cap)


def kernel(x, attn_norm_even, w_in_even, q_norm_a, k_norm_a, rel_bias_a, q_norm_b, k_norm_b, sink_b,
           w_out_even, attn_norm_odd, w_in_odd, q_norm_c, k_norm_c, w_out_odd,
           ffn_norm, w_router, w_gate, w_up, w_down):
    batch, seq, _ = x.shape
    m = batch * seq
    scale = HEAD_DIM ** -0.5 * LOG2E
    x2d = x.reshape(m, D_MODEL)
    depth = ffn_norm.shape[0]
    for layer in range(depth):
        i = layer // 2
        if layer % 2 == 0:
            cos_t, sin_t = _rope_tables(seq)
            colgain = _col_gain([(q_norm_a[i], A_HEADS, scale), (k_norm_a[i], A_HEADS, 1.0),
                                 (None, A_HEADS, 1.0), (q_norm_b[i], B_HEADS, scale),
                                 (k_norm_b[i], B_KV_HEADS, 1.0), (None, B_KV_HEADS, 1.0)])
            sections = (_chunks(0, A_W, "norm", 0) + _chunks(A_W, A_W, "norm", 1)
                        + _chunks(2 * A_W, A_W, "plain", 2) + _chunks(3 * A_W, B_QW, "rope", 3)
                        + _chunks(3 * A_W + B_QW, B_KVW, "rope", 4)
                        + _chunks(3 * A_W + B_QW + B_KVW, B_KVW, "plain", 5))
            widths = [A_W, A_W, A_W, B_QW, B_KVW, B_KVW]
            outs = _norm_proj(x2d, attn_norm_even[i], w_in_even[i].astype(BF16), colgain,
                              cos_t, sin_t, sections, widths, HEAD_DIM // 2, seq)
            qa, ka, va, qb, kb, vb = [o.reshape(batch, seq, -1) for o in outs]
            out_a = _na_attention(qa, ka, va, rel_bias_a[i])
            out_b = _swa_attention(qb, kb, vb, sink_b[i])
            parts = [out_a.reshape(m, A_W), out_b.reshape(m, B_QW)]
            w_out = w_out_even[i]
        else:
            cos_t, sin_t = _axial_tables(seq)
            colgain = _col_gain([(q_norm_c[i], C_HEADS, scale), (k_norm_c[i], C_KV_HEADS, 1.0),
                                 (None, C_KV_HEADS, 1.0)])
            sections = (_chunks(0, C_QW, "rope", 0) + _chunks(C_QW, C_KVW, "rope", 1)
                        + _chunks(C_QW + C_KVW, C_KVW, "plain", 2))
            widths = [C_QW, C_KVW, C_KVW]
            outs = _norm_proj(x2d, attn_norm_odd[i], w_in_odd[i].astype(BF16), colgain,
                              cos_t, sin_t, sections, widths, HEAD_DIM // 4, seq)
            qc, kc, vc = [o.reshape(batch, seq, -1) for o in outs]
            parts = [_dense_attention(qc, kc, vc).reshape(m, C_QW)]
            w_out = w_out_odd[i]
        x2d = _moe_block(x2d, parts, w_out, ffn_norm[layer], w_router[layer],
                         w_gate, w_up, w_down, layer, batch, seq)
    return x2d.reshape(batch, seq, D_MODEL)
```
